```python
import math
import jax
import jax.numpy as jnp
from jax import lax
import numpy as np

D_MODEL = 1024
BATCH = 2
SEQ = 16384
DEPTH = 4

GRID_W = 64
CTX_LEN = 256
N_ADA = 9
ROPE_DIM = 64
ROPE_BASE = 10000.0
NORM_EPS = 1e-6
DA_HEADS = 4
DA_HEAD_DIM = ROPE_DIM
DA_V_DIM = 2 * DA_HEAD_DIM
DA_QK_WIDTH = DA_HEADS * 2 * DA_HEAD_DIM
DA_WIDTH = DA_HEADS * DA_V_DIM
Q_BLOCK = 128
RET_HEADS = 4
RET_QK_DIM = ROPE_DIM
RET_V_DIM = 128
RET_QK_WIDTH = RET_HEADS * RET_QK_DIM
RET_WIDTH = RET_HEADS * RET_V_DIM
RET_CHUNK = 128
CONV_CH = 512
CONV_WIDTH = 31
N_BRANCH = 3
D_FF = 2816
IN_SPLITS = (DA_QK_WIDTH, DA_QK_WIDTH, DA_WIDTH, RET_QK_WIDTH, RET_QK_WIDTH, RET_WIDTH, RET_WIDTH, 2 * CONV_CH, N_BRANCH * D_MODEL)
D_IN = 2 * DA_QK_WIDTH + DA_WIDTH + 2 * RET_QK_WIDTH + 2 * RET_WIDTH + 2 * CONV_CH + N_BRANCH * D_MODEL

kernel_name = 'hybrid_diffusion_backbone'


def rmsnorm(x, g):
    xf = x.astype(jnp.float32)
    y = xf * lax.rsqrt(jnp.mean(xf * xf, axis=-1, keepdims=True) + NORM_EPS)
    return (y * g.astype(jnp.float32)).astype(x.dtype)


def layernorm(x, g, b):
    xf = x.astype(jnp.float32)
    mu = jnp.mean(xf, axis=-1, keepdims=True)
    var = jnp.mean(jnp.square(xf - mu), axis=-1, keepdims=True)
    y = (xf - mu) * lax.rsqrt(var + NORM_EPS)
    return (y * g.astype(jnp.float32) + b.astype(jnp.float32)).astype(x.dtype)


def modulate(x, g_pre, shift, scale):
    return rmsnorm(x, g_pre) * (1 + scale) + shift


def swiglu(h, w_up, w_down):
    a, b = jnp.split(h @ w_up, 2, axis=-1)
    return (jax.nn.silu(a) * b) @ w_down


def ffn_sublayer(x, g_pre, g_post, shift, scale, gate, w_up, w_down):
    return x + 0.5 * gate * rmsnorm(swiglu(modulate(x, g_pre, shift, scale), w_up, w_down), g_post)


def to_heads(t, n_heads):
    b, n, w = t.shape
    return t.reshape(b, n, n_heads, w // n_heads).transpose(0, 2, 1, 3)


def merge_heads(t):
    b, h, n, dh = t.shape
    return t.transpose(0, 2, 1, 3).reshape(b, n, h * dh)


def flip_seq(t):
    return jnp.flip(t, axis=2)


def split_in(t):
    return jnp.split(t, [int(s) for s in np.cumsum(IN_SPLITS)[:-1]], axis=-1)


def axial_rope_tables(rows):
    row = jnp.repeat(jnp.arange(rows, dtype=jnp.float32), GRID_W)
    col = jnp.tile(jnp.arange(GRID_W, dtype=jnp.float32), rows)
    axis_dim = ROPE_DIM // 2
    inv_freq = ROPE_BASE ** (-jnp.arange(0, axis_dim, 2, dtype=jnp.float32) / axis_dim)
    ang = jnp.concatenate([row[:, None] * inv_freq, col[:, None] * inv_freq], axis=-1)
    return jnp.cos(ang), jnp.sin(ang)


def _rotate(x, cos, sin):
    x1, x2 = jnp.split(x, 2, axis=-1)
    return jnp.concatenate([x1 * cos - x2 * sin, x1 * sin + x2 * cos], axis=-1)


def apply_axial_rope(x, cos, sin):
    f = cos.shape[-1] // 2
    cos = cos.astype(x.dtype)
    sin = sin.astype(x.dtype)
    x_row, x_col = jnp.split(x, 2, axis=-1)
    return jnp.concatenate([_rotate(x_row, cos[:, :f], sin[:, :f]), _rotate(x_col, cos[:, f:], sin[:, f:])], axis=-1)


def diff_qk_heads(t):
    b, n, _ = t.shape
    return t.reshape(b, n, DA_HEADS, 2, DA_HEAD_DIM).transpose(0, 2, 3, 1, 4)


def diff_softmax_values(q, k, v, lam):
    s = jnp.einsum('bhcqd,bhckd->bhcqk', q, k) * (DA_HEAD_DIM ** -0.5)
    p = jax.nn.softmax(s.astype(jnp.float32), axis=-1)
    a = p[:, :, 0] - lam * p[:, :, 1]
    return jnp.einsum('bhqk,bhkv->bhqv', a.astype(v.dtype), v)


def diff_attention_latent(q, k, v, k_ctx, v_ctx, lam):
    b, h, _, n, d = q.shape
    nb = n // Q_BLOCK
    k_all = jnp.concatenate([k, k_ctx], axis=3)
    v_all = jnp.concatenate([v, v_ctx], axis=2)
    q_blocks = jnp.moveaxis(q.reshape(b, h, 2, nb, Q_BLOCK, d), 3, 0)
    o = lax.map(lambda qb: diff_softmax_values(qb, k_all, v_all, lam), q_blocks)
    return jnp.moveaxis(o, 0, 2).reshape(b, h, n, DA_V_DIM)


def diff_branch(qx, kx, vx, qc, kc, vc, lam_vecs, subln_g, w_proj, lam_init, cos, sin, need_ctx):
    lv = lam_vecs.astype(jnp.float32)
    lam = jnp.exp(jnp.sum(lv[0] * lv[1])) - jnp.exp(jnp.sum(lv[2] * lv[3])) + lam_init
    qx = apply_axial_rope(diff_qk_heads(qx), cos, sin)
    kx = apply_axial_rope(diff_qk_heads(kx), cos, sin)
    kc = diff_qk_heads(kc)
    vc = to_heads(vc, DA_HEADS)
    ox = diff_attention_latent(qx, kx, to_heads(vx, DA_HEADS), kc, vc, lam)
    out_x = merge_heads(rmsnorm(ox, subln_g) * (1 - lam_init)) @ w_proj
    if not need_ctx:
        return out_x, None
    oc = diff_softmax_values(diff_qk_heads(qc), kc, vc, lam)
    return out_x, merge_heads(rmsnorm(oc, subln_g) * (1 - lam_init)) @ w_proj


def retention_scan(q, k, v, log_gamma, s0, strict):
    b, h, n, dk = q.shape
    dv = v.shape[-1]
    nc = n // RET_CHUNK
    i = jnp.arange(RET_CHUNK, dtype=jnp.float32)
    diff = i[:, None] - i[None, :]
    mask = diff > 0 if strict else diff >= 0
    decay = jnp.where(mask, jnp.exp(log_gamma[:, None, None] * jnp.maximum(diff, 0.0)), 0.0).astype(v.dtype)
    qc = q.reshape(b, h, nc, RET_CHUNK, dk)
    kc = k.reshape(b, h, nc, RET_CHUNK, dk)
    vc = v.reshape(b, h, nc, RET_CHUNK, dv)
    scores = jnp.einsum('bhncd,bhnsd->bhncs', qc, kc) * decay[None, :, None]
    o_intra = jnp.einsum('bhncs,bhnsv->bhncv', scores, vc)
    zeta = jnp.exp(log_gamma[:, None] * (RET_CHUNK - 1 - i)).astype(v.dtype)
    kv = jnp.einsum('bhnsd,hs,bhnsv->nbhdv', kc, zeta, vc)
    chunk_decay = jnp.exp(log_gamma * RET_CHUNK).astype(v.dtype)[None, :, None, None]

    def step(state, kv_n):
        return chunk_decay * state + kv_n, state

    s_final, s_prev = lax.scan(step, s0.astype(kv.dtype), kv)
    xi = jnp.exp(log_gamma[:, None] * (i + 1)).astype(v.dtype)
    o_cross = jnp.einsum('bhncd,hc,nbhdv->bhncv', qc, xi, s_prev)
    return (o_intra + o_cross).reshape(b, h, n, dv), s_final


def retention_state(k, v, log_gamma):
    n = k.shape[2]
    w = jnp.exp(log_gamma[:, None] * (n - 1 - jnp.arange(n, dtype=jnp.float32))).astype(v.dtype)
    return jnp.einsum('bhnd,hn,bhnv->bhdv', k, w, v)


def head_groupnorm(o, g, b):
    of = o.astype(jnp.float32)
    mu = jnp.mean(of, axis=-1, keepdims=True)
    var = jnp.mean(jnp.square(of - mu), axis=-1, keepdims=True)
    y = merge_heads((of - mu) * lax.rsqrt(var + NORM_EPS))
    return (y * g.astype(jnp.float32) + b.astype(jnp.float32)).astype(o.dtype)


def retention_branch(qx, kx, vx, gx, qc, kc, vc, gc, decay_logits, gn_g, gn_b, w_proj, cos, sin, need_ctx):
    lg = jax.nn.log_sigmoid(decay_logits.astype(jnp.float32))
    scale = RET_QK_DIM ** -0.5
    qx = apply_axial_rope(to_heads(qx, RET_HEADS), cos, sin) * scale
    kx = apply_axial_rope(to_heads(kx, RET_HEADS), cos, sin)
    vx = to_heads(vx, RET_HEADS)
    kc = to_heads(kc, RET_HEADS)
    vc = to_heads(vc, RET_HEADS)
    if need_ctx:
        qc = to_heads(qc, RET_HEADS) * scale
        s0 = jnp.zeros((kc.shape[0], RET_HEADS, RET_QK_DIM, RET_V_DIM), vc.dtype)
        oc_f, sc_f = retention_scan(qc, kc, vc, lg[0], s0, False)
        oc_b, sc_b = retention_scan(flip_seq(qc), flip_seq(kc), flip_seq(vc), lg[1], s0, True)
    else:
        sc_f = retention_state(kc, vc, lg[0])
        sc_b = retention_state(flip_seq(kc), flip_seq(vc), lg[1])
    ox_f, _ = retention_scan(qx, kx, vx, lg[0], sc_f, False)
    ox_b, _ = retention_scan(flip_seq(qx), flip_seq(kx), flip_seq(vx), lg[1], sc_b, True)
    out_x = (head_groupnorm(ox_f + flip_seq(ox_b), gn_g, gn_b) * jax.nn.silu(gx)) @ w_proj
    if not need_ctx:
        return out_x, None
    out_c = (head_groupnorm(oc_f + flip_seq(oc_b), gn_g, gn_b) * jax.nn.silu(gc)) @ w_proj
    return out_x, out_c


def conformer_conv(u, dw, dw_b, ln_g, ln_b, w_proj):
    a, gate = jnp.split(u, 2, axis=-1)
    u = a * jax.nn.sigmoid(gate)
    u = lax.conv_general_dilated(u, dw[:, None, :].astype(u.dtype), window_strides=(1,),
                                 padding=((CONV_WIDTH // 2, CONV_WIDTH // 2),),
                                 dimension_numbers=('NWC', 'WIO', 'NWC'),
                                 feature_group_count=CONV_CH) + dw_b
    return jax.nn.silu(layernorm(u, ln_g, ln_b)) @ w_proj


def gated_merge(gate_logits, y_a, y_r, y_c, w_out):
    g_a, g_r, g_c = jnp.split(jax.nn.sigmoid(gate_logits), N_BRANCH, axis=-1)
    return (g_a * y_a + g_r * y_r + g_c * y_c) @ w_out


def token_mixer(hx, hc, p, lam_init, cos, sin, need_ctx):
    da_qx, da_kx, da_vx, r_qx, r_kx, r_vx, r_gx, cv_x, gt_x = split_in(hx @ p['w_in'])
    da_qc, da_kc, da_vc, r_qc, r_kc, r_vc, r_gc, cv_c, gt_c = split_in(hc @ p['w_in'])
    ya_x, ya_c = diff_branch(da_qx, da_kx, da_vx, da_qc, da_kc, da_vc, p['da_lambda'], p['da_subln'],
                             p['da_proj'], lam_init, cos, sin, need_ctx)
    yr_x, yr_c = retention_branch(r_qx, r_kx, r_vx, r_gx, r_qc, r_kc, r_vc, r_gc, p['ret_decay'],
                                  p['ret_gn_g'], p['ret_gn_b'], p['ret_proj'], cos, sin, need_ctx)
    conv_args = (p['conv_dw'], p['conv_dw_b'], p['conv_ln_g'], p['conv_ln_b'], p['conv_proj'])
    out_x = gated_merge(gt_x, ya_x, yr_x, conformer_conv(cv_x, *conv_args), p['w_out'])
    if not need_ctx:
        return out_x, None
    out_c = gated_merge(gt_c, ya_c, yr_c, conformer_conv(cv_c, *conv_args), p['w_out'])
    return out_x, out_c


def setup_inputs(seed: int = 0) -> dict:
    key = jax.random.key(seed)
    ks = iter(jax.random.split(key, 26))

    def nrm(shape, scale=1.0):
        return scale * jax.random.normal(next(ks), shape, jnp.float32)

    def gain(shape):
        return 1.0 + nrm(shape, 0.05)

    gamma = 1.0 - 2.0 ** (-5.0 - jnp.arange(RET_HEADS, dtype=jnp.float32))
    decay_logit = jnp.log(gamma) - jnp.log1p(-gamma)
    return {
        'x': nrm((BATCH, SEQ, D_MODEL)),
        'c': nrm((BATCH, D_MODEL)),
        'ctx': nrm((BATCH, CTX_LEN, D_MODEL)),
        'c_ctx': nrm((D_MODEL,)),
        'ada_w': nrm((DEPTH, D_MODEL, N_ADA * D_MODEL), 0.5 * D_MODEL ** -0.5),
        'ada_b': nrm((DEPTH, N_ADA * D_MODEL), 0.02),
        'norm_pre': gain((DEPTH, 3, D_MODEL)),
        'norm_post': gain((DEPTH, 3, D_MODEL)),
        'ffn1_up': nrm((DEPTH, D_MODEL, 2 * D_FF), D_MODEL ** -0.5),
        'ffn1_down': nrm((DEPTH, D_FF, D_MODEL), D_FF ** -0.5),
        'ffn2_up': nrm((DEPTH, D_MODEL, 2 * D_FF), D_MODEL ** -0.5),
        'ffn2_down': nrm((DEPTH, D_FF, D_MODEL), D_FF ** -0.5),
        'w_in': nrm((DEPTH, D_MODEL, D_IN), D_MODEL ** -0.5),
        'da_lambda': nrm((DEPTH, 4, DA_HEAD_DIM), 0.1),
        'da_subln': gain((DEPTH, DA_V_DIM)),
        'da_proj': nrm((DEPTH, DA_WIDTH, D_MODEL), DA_WIDTH ** -0.5),
        'ret_decay': decay_logit + nrm((DEPTH, 2, RET_HEADS), 0.1),
        'ret_gn_g': gain((DEPTH, RET_WIDTH)),
        'ret_gn_b': nrm((DEPTH, RET_WIDTH), 0.02),
        'ret_proj': nrm((DEPTH, RET_WIDTH, D_MODEL), RET_WIDTH ** -0.5),
        'conv_dw': nrm((DEPTH, CONV_WIDTH, CONV_CH), CONV_WIDTH ** -0.5),
        'conv_dw_b': nrm((DEPTH, CONV_CH), 0.02),
        'conv_ln_g': gain((DEPTH, CONV_CH)),
        'conv_ln_b': nrm((DEPTH, CONV_CH), 0.02),
        'conv_proj': nrm((DEPTH, CONV_CH, D_MODEL), CONV_CH ** -0.5),
        'w_out': nrm((DEPTH, D_MODEL, D_MODEL), D_MODEL ** -0.5),
    }


def reference(x, c, ctx, c_ctx, ada_w, ada_b, norm_pre, norm_post, ffn1_up, ffn1_down, ffn2_up, ffn2_down,
              w_in, da_lambda, da_subln, da_proj, ret_decay, ret_gn_g, ret_gn_b, ret_proj,
              conv_dw, conv_dw_b, conv_ln_g, conv_ln_b, conv_proj, w_out):
    rows = x.shape[1] // GRID_W
    cos, sin = axial_rope_tables(rows)
    sc = jax.nn.silu(c)
    scc = jax.nn.silu(c_ctx)
    for l in range(DEPTH):
        need_ctx = l < DEPTH - 1
        lam_init = 0.8 - 0.6 * math.exp(-0.3 * l)
        mx = jnp.split((sc @ ada_w[l] + ada_b[l])[:, None, :], N_ADA, axis=-1)
        mc = jnp.split(scc @ ada_w[l] + ada_b[l], N_ADA, axis=-1)
        x = ffn_sublayer(x, norm_pre[l, 0], norm_post[l, 0], mx[0], mx[1], mx[2], ffn1_up[l], ffn1_down[l])
        ctx = ffn_sublayer(ctx, norm_pre[l, 0], norm_post[l, 0], mc[0], mc[1], mc[2], ffn1_up[l], ffn1_down[l])
        p = {'w_in': w_in[l], 'da_lambda': da_lambda[l], 'da_subln': da_subln[l], 'da_proj': da_proj[l],
             'ret_decay': ret_decay[l], 'ret_gn_g': ret_gn_g[l], 'ret_gn_b': ret_gn_b[l], 'ret_proj': ret_proj[l],
             'conv_dw': conv_dw[l], 'conv_dw_b': conv_dw_b[l], 'conv_ln_g': conv_ln_g[l], 'conv_ln_b': conv_ln_b[l],
             'conv_proj': conv_proj[l], 'w_out': w_out[l]}
        hx = modulate(x, norm_pre[l, 1], mx[3], mx[4])
        hc = modulate(ctx, norm_pre[l, 1], mc[3], mc[4])
        yx, yc = token_mixer(hx, hc, p, lam_init, cos, sin, need_ctx)
        x = x + mx[5] * rmsnorm(yx, norm_post[l, 1])
        x = ffn_sublayer(x, norm_pre[l, 2], norm_post[l, 2], mx[6], mx[7], mx[8], ffn2_up[l], ffn2_down[l])
        if need_ctx:
            ctx = ctx + mc[5] * rmsnorm(yc, norm_post[l, 1])
            ctx = ffn_sublayer(ctx, norm_pre[l, 2], norm_post[l, 2], mc[6], mc[7], mc[8], ffn2_up[l], ffn2_down[l])
    return x
```

```python
import functools
import math

import jax
import jax.numpy as jnp
from jax import lax
from jax.experimental import pallas as pl
from jax.experimental.pallas import tpu as pltpu

F32 = jnp.float32
BF16 = jnp.bfloat16

D_MODEL = 1024
N_ADA = 9
GRID_W = 64
ROPE_DIM = 64
ROPE_BASE = 10000.0
NORM_EPS = 1e-6
HEADS = 4
HEAD_LANES = 128
RET_CHUNK = 128
CONV_CH = 512
CONV_WIDTH = 31
CONV_HALO = 16
D_FF = 2816
MIX_W = 512
N_BRANCH = 3
_OFF = {}
_o = 0
for _name, _w in (("da_q", 512), ("da_k", 512), ("da_v", 512), ("r_q", 256), ("r_k", 256), ("r_v", 512),
                  ("r_g", 512), ("cv_a", 512), ("cv_g", 512), ("gt", N_BRANCH * D_MODEL)):
    _OFF[_name] = (_o, _o + _w)
    _o += _w
D_IN = _o

V7X_VMEM_LIMIT = 56 * 1024 * 1024


def _params(semantics, vmem=V7X_VMEM_LIMIT):
    return pltpu.CompilerParams(dimension_semantics=semantics, vmem_limit_bytes=vmem)


def _resident(shape, index_map):
    return pl.BlockSpec(shape, index_map, pipeline_mode=pl.Buffered(1))


def _sigmoid(x):
    return 1.0 / (1.0 + jnp.exp(-x))


def _rms(x, g):
    return x * lax.rsqrt(jnp.mean(x * x, axis=-1, keepdims=True) + NORM_EPS) * g


def _modulated(x, g_pre, mod_ref, k0):
    shift = mod_ref[0, k0:k0 + 1, :]
    scale = mod_ref[0, k0 + 1:k0 + 2, :]
    return _rms(x, g_pre) * (1.0 + scale) + shift


def _ada_kernel(c_ref, w_ref, b_ref, o_ref):
    c = c_ref[...]
    sc = c * _sigmoid(c)
    o_ref[0] = jnp.dot(sc, w_ref[0], preferred_element_type=F32) + b_ref[0]


def _ada(cvec, ada_w, ada_b):
    depth, d, n = ada_w.shape
    tn = 1024
    return pl.pallas_call(
        _ada_kernel,
        grid=(depth, n // tn),
        in_specs=[pl.BlockSpec((8, d), lambda l, j: (0, 0)),
                  pl.BlockSpec((1, d, tn), lambda l, j: (l, 0, j)),
                  pl.BlockSpec((1, 1, tn), lambda l, j: (l, 0, j))],
        out_specs=pl.BlockSpec((1, 8, tn), lambda l, j: (l, 0, j)),
        out_shape=jax.ShapeDtypeStruct((depth, 8, n), F32),
        name="ada",
        compiler_params=_params(("parallel", "parallel")),
    )(cvec, ada_w, ada_b.reshape(depth, 1, n))


def _ffn_kernel(x_ref, mod_ref, gpre_ref, gpost_ref, wup_ref, wdn_ref, o_ref, *, k0):
    x = x_ref[0]
    h = _modulated(x, gpre_ref[...], mod_ref, k0).astype(BF16)
    ab = jnp.dot(h, wup_ref[...], preferred_element_type=F32)
    a = ab[:, :D_FF]
    b = ab[:, D_FF:]
    u = (a * _sigmoid(a) * b).astype(BF16)
    y = jnp.dot(u, wdn_ref[...], preferred_element_type=F32)
    gate = mod_ref[0, k0 + 2:k0 + 3, :]
    o_ref[0] = x + 0.5 * gate * _rms(y, gpost_ref[...])


def _ffn(x, mods, mod_row, g_pre, g_post, w_up, w_dn, layer, k0):
    b, n, d = x.shape
    tm = min(256, n)
    row = (lambda bi: bi) if mod_row is None else (lambda bi: mod_row)
    return pl.pallas_call(
        functools.partial(_ffn_kernel, k0=k0),
        grid=(b, n // tm),
        in_specs=[pl.BlockSpec((1, tm, d), lambda bi, i: (bi, i, 0)),
                  pl.BlockSpec((1, N_ADA, d), lambda bi, i: (row(bi), 0, 0)),
                  pl.BlockSpec((1, d), lambda bi, i: (0, 0)),
                  pl.BlockSpec((1, d), lambda bi, i: (0, 0)),
                  _resident((None, d, 2 * D_FF), lambda bi, i: (layer, 0, 0)),
                  _resident((None, D_FF, d), lambda bi, i: (layer, 0, 0))],
        out_specs=pl.BlockSpec((1, tm, d), lambda bi, i: (bi, i, 0)),
        out_shape=jax.ShapeDtypeStruct(x.shape, F32),
        name="ffn",
        compiler_params=_params(("parallel", "parallel")),
    )(x, mods, g_pre, g_post, w_up, w_dn)


def _swap16(c):
    lane = lax.broadcasted_iota(jnp.int32, c.shape, 1)
    return jnp.where((lane & 16) != 0, pltpu.roll(c, 16, 1), pltpu.roll(c, 112, 1))


def _inproj_kernel(*refs, rope):
    if rope:
        x_ref, mod_ref, gpre_ref, w_ref, cos_ref, sin_ref = refs[:6]
        outs = refs[6:]
    else:
        x_ref, mod_ref, gpre_ref, w_ref = refs[:4]
        outs = refs[4:]
    daq_ref, dak_ref, dav_ref, rq_ref, rk_ref, rv_ref, rg_ref, cu_ref, gt_ref = outs
    h = _modulated(x_ref[0], gpre_ref[...], mod_ref, 3).astype(BF16)
    y = jnp.dot(h, w_ref[...], preferred_element_type=F32)

    def rotary(name, out_ref, mul):
        lo, hi = _OFF[name]
        for j in range((hi - lo) // 128):
            c = y[:, lo + 128 * j:lo + 128 * (j + 1)]
            if rope:
                c = c * cos_ref[...] + _swap16(c) * sin_ref[...]
            if mul != 1.0:
                c = c * mul
            out_ref[0, :, 128 * j:128 * (j + 1)] = c.astype(BF16)

    def plain(name, out_ref):
        lo, hi = _OFF[name]
        out_ref[0] = y[:, lo:hi].astype(BF16)

    rotary("da_q", daq_ref, 1.0)
    rotary("da_k", dak_ref, 1.0)
    plain("da_v", dav_ref)
    rotary("r_q", rq_ref, ROPE_DIM ** -0.5)
    rotary("r_k", rk_ref, 1.0)
    plain("r_v", rv_ref)
    plain("r_g", rg_ref)
    a = y[:, _OFF["cv_a"][0]:_OFF["cv_a"][1]]
    g = y[:, _OFF["cv_g"][0]:_OFF["cv_g"][1]]
    cu_ref[0] = (a * _sigmoid(g)).astype(BF16)
    plain("gt", gt_ref)


_INPROJ_WIDTHS = (512, 512, 512, 256, 256, 512, 512, 512, N_BRANCH * D_MODEL)


def _inproj(x, mods, mod_row, g_pre, w_in, layer, tables):
    b, n, d = x.shape
    tm = min(256, n)
    rope = tables is not None
    row = (lambda bi: bi) if mod_row is None else (lambda bi: mod_row)
    in_specs = [pl.BlockSpec((1, tm, d), lambda bi, i: (bi, i, 0)),
                pl.BlockSpec((1, N_ADA, d), lambda bi, i: (row(bi), 0, 0)),
                pl.BlockSpec((1, d), lambda bi, i: (0, 0)),
                _resident((None, d, D_IN), lambda bi, i: (layer, 0, 0))]
    args = [x, mods, g_pre, w_in]
    if rope:
        in_specs += [pl.BlockSpec((tm, 128), lambda bi, i: (i, 0))] * 2
        args += list(tables)
    return pl.pallas_call(
        functools.partial(_inproj_kernel, rope=rope),
        grid=(b, n // tm),
        in_specs=in_specs,
        out_specs=[pl.BlockSpec((1, tm, w), lambda bi, i: (bi, i, 0)) for w in _INPROJ_WIDTHS],
        out_shape=[jax.ShapeDtypeStruct((b, n, w), BF16) for w in _INPROJ_WIDTHS],
        name="inproj",
        compiler_params=_params(("parallel", "parallel")),
    )(*args)


def _rope_tables(n):
    t = jnp.arange(n, dtype=jnp.int32)
    row = (t // GRID_W).astype(F32)
    col = (t % GRID_W).astype(F32)
    axis_dim = ROPE_DIM // 2
    inv_freq = ROPE_BASE ** (-jnp.arange(0, axis_dim, 2, dtype=F32) / axis_dim)
    ang_r = row[:, None] * inv_freq
    ang_c = col[:, None] * inv_freq
    cos64 = jnp.concatenate([jnp.cos(ang_r), jnp.cos(ang_r), jnp.cos(ang_c), jnp.cos(ang_c)], axis=-1)
    sin64 = jnp.concatenate([-jnp.sin(ang_r), jnp.sin(ang_r), -jnp.sin(ang_c), jnp.sin(ang_c)], axis=-1)
    return jnp.tile(cos64, (1, 2)), jnp.tile(sin64, (1, 2))


def _attn_kernel(*refs, n_blocks, tk, lam_init):
    if n_blocks:
        lam_ref, subln_ref, q_ref, kx_ref, vx_ref, kc_ref, vc_ref, o_ref, m_ref, l_ref, acc_ref = refs
    else:
        lam_ref, subln_ref, q_ref, kc_ref, vc_ref, o_ref, m_ref, l_ref, acc_ref = refs
    q = q_ref[0]
    lane = lax.broadcasted_iota(jnp.int32, q.shape, 1)
    qs = q * jnp.asarray(ROPE_DIM ** -0.5, BF16)
    zero = jnp.zeros_like(qs)
    qmaps = (jnp.where(lane < ROPE_DIM, qs, zero), jnp.where(lane >= ROPE_DIM, qs, zero))

    def scores(c, k):
        return lax.dot_general(qmaps[c], k, (((1,), (1,)), ((), ())), preferred_element_type=F32)

    kc = kc_ref[0]
    vc = vc_ref[0]
    for c in range(2):
        s = scores(c, kc)
        m = jnp.max(s, axis=-1, keepdims=True)
        p = jnp.exp(s - m)
        m_ref[c] = m
        l_ref[c] = jnp.sum(p, axis=-1, keepdims=True)
        acc_ref[c] = jnp.dot(p.astype(BF16), vc, preferred_element_type=F32)

    if n_blocks:
        def body(j, carry):
            start = pl.multiple_of(j * tk, tk)
            k = kx_ref[0, pl.ds(start, tk), :]
            v = vx_ref[0, pl.ds(start, tk), :]
            for c in range(2):
                s = scores(c, k)
                m_old = m_ref[c]
                m_new = jnp.maximum(m_old, jnp.max(s, axis=-1, keepdims=True))
                alpha = jnp.exp(m_old - m_new)
                p = jnp.exp(s - m_new)
                l_ref[c] = alpha * l_ref[c] + jnp.sum(p, axis=-1, keepdims=True)
                acc_ref[c] = alpha * acc_ref[c] + jnp.dot(p.astype(BF16), v, preferred_element_type=F32)
                m_ref[c] = m_new
            return carry

        lax.fori_loop(0, n_blocks, body, 0)

    lv = lam_ref[...]
    lam = (jnp.exp(jnp.sum(lv[0:1] * lv[1:2], axis=-1, keepdims=True))
           - jnp.exp(jnp.sum(lv[2:3] * lv[3:4], axis=-1, keepdims=True)) + lam_init)
    o = acc_ref[0] / l_ref[0] - lam * (acc_ref[1] / l_ref[1])
    o_ref[0] = (_rms(o, subln_ref[...]) * (1.0 - lam_init)).astype(BF16)


def _attention(q, kx, vx, kc, vc, da_lambda, subln, lam_init):
    b, nq, _ = q.shape
    nc = kc.shape[1]
    tq = min(256, nq)
    hl = HEAD_LANES
    n_blocks, tk = 0, 0
    in_specs = [pl.BlockSpec((4, ROPE_DIM), lambda bi, h, i: (0, 0)),
                pl.BlockSpec((1, hl), lambda bi, h, i: (0, 0)),
                pl.BlockSpec((1, tq, hl), lambda bi, h, i: (bi, i, h))]
    args = [da_lambda, subln, q]
    if kx is not None:
        nk = kx.shape[1]
        tk = min(512, nk)
        n_blocks = nk // tk
        in_specs += [pl.BlockSpec((1, nk, hl), lambda bi, h, i: (bi, 0, h))] * 2
        args += [kx, vx]
    in_specs += [pl.BlockSpec((1, nc, hl), lambda bi, h, i: (bi, 0, h))] * 2
    args += [kc, vc]
    return pl.pallas_call(
        functools.partial(_attn_kernel, n_blocks=n_blocks, tk=tk, lam_init=lam_init),
        grid=(b, HEADS, nq // tq),
        in_specs=in_specs,
        out_specs=pl.BlockSpec((1, tq, hl), lambda bi, h, i: (bi, i, h)),
        out_shape=jax.ShapeDtypeStruct((b, nq, MIX_W), BF16),
        scratch_shapes=[pltpu.VMEM((2, tq, 1), F32), pltpu.VMEM((2, tq, 1), F32), pltpu.VMEM((2, tq, hl), F32)],
        name="attn",
        compiler_params=_params(("parallel", "parallel", "arbitrary")),
    )(*args)


def _ret_kernel(*refs, reverse, chunks):
    if reverse:
        (dec_ref, q_ref, k_ref, v_ref, s0_ref, of_ref, g_ref, gng_ref, gnb_ref,
         o_ref, sfin_ref, state_ref) = refs
    else:
        dec_ref, q_ref, k_ref, v_ref, s0_ref, o_ref, sfin_ref, state_ref = refs
    step = pl.program_id(1)

    @pl.when(step == 0)
    def _():
        state_ref[...] = s0_ref[0]

    cch = RET_CHUNK
    ii = lax.broadcasted_iota(jnp.int32, (cch, cch), 0).astype(F32)
    ss = lax.broadcasted_iota(jnp.int32, (cch, cch), 1).astype(F32)
    pos = lax.broadcasted_iota(jnp.int32, (cch, 1), 0).astype(F32)
    lane = lax.broadcasted_iota(jnp.int32, (cch, 2 * ROPE_DIM), 1)
    d_row = 1 if reverse else 0
    for h in range(HEADS):
        x = dec_ref[d_row:d_row + 1, h:h + 1]
        lg = jnp.minimum(x, 0.0) - jnp.log(1.0 + jnp.exp(-jnp.abs(x)))
        if reverse:
            dist = ss - ii
            dmat = jnp.where(dist > 0, jnp.exp(lg * jnp.maximum(dist, 0.0)), 0.0)
            zeta = jnp.exp(lg * pos)
            xi = jnp.exp(lg * (cch - pos))
        else:
            dist = ii - ss
            dmat = jnp.where(dist >= 0, jnp.exp(lg * jnp.maximum(dist, 0.0)), 0.0)
            zeta = jnp.exp(lg * (cch - 1 - pos))
            xi = jnp.exp(lg * (pos + 1.0))
        chunk_decay = jnp.exp(lg * cch)
        pair = slice(128 * (h // 2), 128 * (h // 2 + 1))
        vcol = slice(128 * h, 128 * (h + 1))
        lo = ROPE_DIM * (h % 2)
        head_lanes = (lane >= lo) & (lane < lo + ROPE_DIM)
        order = range(chunks - 1, -1, -1) if reverse else range(chunks)
        for c in order:
            rows = slice(cch * c, cch * (c + 1))
            qc = q_ref[0, rows, pair]
            kc = k_ref[0, rows, pair]
            kc = jnp.where(head_lanes, kc, jnp.zeros_like(kc))
            vc = v_ref[0, rows, vcol]
            sc = lax.dot_general(qc, kc, (((1,), (1,)), ((), ())), preferred_element_type=F32) * dmat
            o = jnp.dot(sc.astype(BF16), vc, preferred_element_type=F32)
            state = state_ref[h]
            o = o + xi * jnp.dot(qc, state.astype(BF16), preferred_element_type=F32)
            vz = (vc.astype(F32) * zeta).astype(BF16)
            kv = lax.dot_general(kc, vz, (((0,), (0,)), ((), ())), preferred_element_type=F32)
            state_ref[h] = chunk_decay * state + kv
            if reverse:
                t = o + of_ref[0, rows, vcol]
                mu = jnp.mean(t, axis=-1, keepdims=True)
                tc = t - mu
                var = jnp.mean(tc * tc, axis=-1, keepdims=True)
                y = tc * lax.rsqrt(var + NORM_EPS) * gng_ref[:, vcol] + gnb_ref[:, vcol]
                g = g_ref[0, rows, vcol].astype(F32)
                o_ref[0, rows, vcol] = (y * (g * _sigmoid(g))).astype(BF16)
            else:
                o_ref[0, rows, vcol] = o

    @pl.when(step == pl.num_programs(1) - 1)
    def _():
        sfin_ref[0] = state_ref[...]


def _retention_pass(decay, q, k, v, s0, post=None):
    b, n, _ = q.shape
    t = min(512, n)
    steps = n // t
    reverse = post is not None
    blk = (lambda bi, i: (bi, steps - 1 - i, 0)) if reverse else (lambda bi, i: (bi, i, 0))
    const2 = lambda bi, i: (0, 0)
    state_spec = pl.BlockSpec((1, HEADS, 128, 128), lambda bi, i: (bi, 0, 0, 0))
    in_specs = [pl.BlockSpec((2, HEADS), const2),
                pl.BlockSpec((1, t, 256), blk), pl.BlockSpec((1, t, 256), blk), pl.BlockSpec((1, t, MIX_W), blk),
                state_spec]
    args = [decay, q, k, v, s0]
    if reverse:
        o_fwd, gate, gn_g, gn_b = post
        in_specs += [pl.BlockSpec((1, t, MIX_W), blk), pl.BlockSpec((1, t, MIX_W), blk),
                     pl.BlockSpec((1, MIX_W), const2), pl.BlockSpec((1, MIX_W), const2)]
        args += [o_fwd, gate, gn_g, gn_b]
    return pl.pallas_call(
        functools.partial(_ret_kernel, reverse=reverse, chunks=t // RET_CHUNK),
        grid=(b, steps),
        in_specs=in_specs,
        out_specs=[pl.BlockSpec((1, t, MIX_W), blk), state_spec],
        out_shape=[jax.ShapeDtypeStruct((b, n, MIX_W), BF16 if reverse else F32),
                   jax.ShapeDtypeStruct((b, HEADS, 128, 128), F32)],
        scratch_shapes=[pltpu.VMEM((HEADS, 128, 128), F32)],
        name="retention_bwd" if reverse else "retention_fwd",
        compiler_params=_params(("parallel", "arbitrary")),
    )(*args)


def _conv_kernel(prev_ref, cur_ref, next_ref, dw_ref, dwb_ref, lng_ref, lnb_ref, o_ref, buf_ref, *, t):
    i = pl.program_id(1)
    halo = CONV_HALO
    first = i == 0
    last = i == pl.num_programs(1) - 1
    prev = prev_ref[0].astype(F32)
    nxt = next_ref[0].astype(F32)
    buf_ref[0:halo, :] = jnp.where(first, jnp.zeros_like(prev), prev)
    buf_ref[halo:halo + t, :] = cur_ref[0].astype(F32)
    buf_ref[halo + t:2 * halo + t, :] = jnp.where(last, jnp.zeros_like(nxt), nxt)
    rb = 32
    off = halo - CONV_WIDTH // 2
    for r in range(t // rb):
        acc = jnp.broadcast_to(dwb_ref[...], (rb, CONV_CH))
        for w in range(CONV_WIDTH):
            acc = acc + buf_ref[rb * r + off + w:rb * r + off + w + rb, :] * dw_ref[w:w + 1, :]
        mu = jnp.mean(acc, axis=-1, keepdims=True)
        ac = acc - mu
        var = jnp.mean(ac * ac, axis=-1, keepdims=True)
        y = ac * lax.rsqrt(var + NORM_EPS) * lng_ref[...] + lnb_ref[...]
        o_ref[0, rb * r:rb * (r + 1), :] = (y * _sigmoid(y)).astype(BF16)


def _conv(u, dw, dw_b, ln_g, ln_b):
    b, n, ch = u.shape
    t = min(256, n)
    hb = t // CONV_HALO
    nh = n // CONV_HALO
    const2 = lambda bi, i: (0, 0)
    return pl.pallas_call(
        functools.partial(_conv_kernel, t=t),
        grid=(b, n // t),
        in_specs=[pl.BlockSpec((1, CONV_HALO, ch), lambda bi, i: (bi, jnp.maximum(i * hb - 1, 0), 0)),
                  pl.BlockSpec((1, t, ch), lambda bi, i: (bi, i, 0)),
                  pl.BlockSpec((1, CONV_HALO, ch), lambda bi, i: (bi, jnp.minimum((i + 1) * hb, nh - 1), 0)),
                  pl.BlockSpec((CONV_WIDTH, ch), const2),
                  pl.BlockSpec((1, ch), const2), pl.BlockSpec((1, ch), const2), pl.BlockSpec((1, ch), const2)],
        out_specs=pl.BlockSpec((1, t, ch), lambda bi, i: (bi, i, 0)),
        out_shape=jax.ShapeDtypeStruct((b, n, ch), BF16),
        scratch_shapes=[pltpu.VMEM((t + 2 * CONV_HALO, ch), F32)],
        name="conv",
        compiler_params=_params(("parallel", "parallel")),
    )(u, u, u, dw, dw_b, ln_g, ln_b)


def _merge_kernel(x_ref, mod_ref, gpost_ref, oa_ref, yr_ref, yc_ref, gt_ref,
                  wa_ref, wr_ref, wc_ref, wo_ref, o_ref):
    d = D_MODEL
    m = None
    for j, (y_ref, w_ref) in enumerate(((oa_ref, wa_ref), (yr_ref, wr_ref), (yc_ref, wc_ref))):
        y = jnp.dot(y_ref[0], w_ref[...], preferred_element_type=F32)
        term = _sigmoid(gt_ref[0, :, d * j:d * (j + 1)].astype(F32)) * y
        m = term if m is None else m + term
    y = jnp.dot(m.astype(BF16), wo_ref[...], preferred_element_type=F32)
    gate = mod_ref[0, 5:6, :]
    o_ref[0] = x_ref[0] + gate * _rms(y, gpost_ref[...])


def _merge(x, mods, mod_row, g_post, oa, yr, yc, gt, w_a, w_r, w_c, w_o, layer):
    b, n, d = x.shape
    tm = min(256, n)
    row = (lambda bi: bi) if mod_row is None else (lambda bi: mod_row)
    tok = lambda w: pl.BlockSpec((1, tm, w), lambda bi, i: (bi, i, 0))
    wsp = lambda k: _resident((None, k, d), lambda bi, i: (layer, 0, 0))
    return pl.pallas_call(
        _merge_kernel,
        grid=(b, n // tm),
        in_specs=[tok(d), pl.BlockSpec((1, N_ADA, d), lambda bi, i: (row(bi), 0, 0)),
                  pl.BlockSpec((1, d), lambda bi, i: (0, 0)),
                  tok(MIX_W), tok(MIX_W), tok(MIX_W), tok(N_BRANCH * d),
                  wsp(MIX_W), wsp(MIX_W), wsp(MIX_W), wsp(d)],
        out_specs=tok(d),
        out_shape=jax.ShapeDtypeStruct(x.shape, F32),
        name="merge",
        compiler_params=_params(("parallel", "parallel")),
    )(x, mods, g_post, oa, yr, yc, gt, w_a, w_r, w_c, w_o)


def kernel(x, c, ctx, c_ctx, ada_w, ada_b, norm_pre, norm_post, ffn1_up, ffn1_down, ffn2_up, ffn2_down, w_in, da_lambda, da_subln, da_proj, ret_decay, ret_gn_g, ret_gn_b, ret_proj, conv_dw, conv_dw_b, conv_ln_g, conv_ln_b, conv_proj, w_out):
    depth = ada_w.shape[0]
    b, n, d = x.shape
    ctx_row = b
    cvec = jnp.zeros((8, d), F32).at[:b].set(c).at[ctx_row].set(c_ctx)
    mods = _ada(cvec, ada_w, ada_b).reshape(depth, 8, N_ADA, d)
    tables = _rope_tables(n)
    wb = lambda w: w.astype(BF16)
    ffn1_up, ffn1_down, ffn2_up, ffn2_down = wb(ffn1_up), wb(ffn1_down), wb(ffn2_up), wb(ffn2_down)
    w_in, da_proj, ret_proj, conv_proj, w_out = wb(w_in), wb(da_proj), wb(ret_proj), wb(conv_proj), wb(w_out)
    zero_state = jnp.zeros((b, HEADS, 128, 128), F32)

    for l in range(depth):
        lam_init = 0.8 - 0.6 * math.exp(-0.3 * l)
        mod_l = mods[l]
        row = lambda a, j: a[l, j][None, :]
        streams = []
        for tok, mrow, tab in ((ctx, ctx_row, None), (x, None, tables)):
            tok = _ffn(tok, mod_l, mrow, row(norm_pre, 0), row(norm_post, 0), ffn1_up, ffn1_down, l, 0)
            proj = _inproj(tok, mod_l, mrow, row(norm_pre, 1), w_in, l, tab)
            streams.append((tok, mrow, proj))
        (ctx, _, pc), (x, _, px) = streams
        subln = da_subln[l][None, :]
        dec = ret_decay[l]
        gn_g, gn_b = ret_gn_g[l][None, :], ret_gn_b[l][None, :]
        conv_args = (conv_dw[l], conv_dw_b[l][None, :], conv_ln_g[l][None, :], conv_ln_b[l][None, :])
        oa_c = _attention(pc[0], None, None, pc[1], pc[2], da_lambda[l], subln, lam_init)
        of_c, sf_c = _retention_pass(dec, pc[3], pc[4], pc[5], zero_state)
        yr_c, sb_c = _retention_pass(dec, pc[3], pc[4], pc[5], zero_state, post=(of_c, pc[6], gn_g, gn_b))
        yc_c = _conv(pc[7], *conv_args)
        oa_x = _attention(px[0], px[1], px[2], pc[1], pc[2], da_lambda[l], subln, lam_init)
        of_x, _ = _retention_pass(dec, px[3], px[4], px[5], sf_c)
        yr_x, _ = _retention_pass(dec, px[3], px[4], px[5], sb_c, post=(of_x, px[6], gn_g, gn_b))
        yc_x = _conv(px[7], *conv_args)
        outs = []
        for tok, mrow, oa, yr, yc, gt in ((ctx, ctx_row, oa_c, yr_c, yc_c, pc[8]), (x, None, oa_x, yr_x, yc_x, px[8])):
            tok = _merge(tok, mod_l, mrow, row(norm_post, 1), oa, yr, yc, gt, da_proj, ret_proj, conv_proj, w_out, l)
            tok = _ffn(tok, mod_l, mrow, row(norm_pre, 2), row(norm_post, 2), ffn2_up, ffn2_down, l, 6)
            outs.append(tok)
        ctx, x = outs
    return x
```

```python
import functools
import math

import jax
import jax.numpy as jnp
from jax import lax
from jax.experimental import pallas as pl
from jax.experimental.pallas import tpu as pltpu

F32 = jnp.float32
BF16 = jnp.bfloat16

D_MODEL = 1024
N_ADA = 9
GRID_W = 64
ROPE_DIM = 64
ROPE_BASE = 10000.0
NORM_EPS = 1e-6
HEADS = 4
HEAD_LANES = 128
ATTN_BLOCK = 256
ATTN_Q_SCALE = ROPE_DIM ** -0.5 * math.log2(math.e)
RET_CHUNK = 128
CONV_CH = 512
CONV_WIDTH = 31
CONV_HALO = 16
D_FF = 2816
MIX_W = 512
N_BRANCH = 3
_OFF = {}
_o = 0
for _name, _w in (("da_q", 512), ("da_k", 512), ("da_v", 512), ("r_q", 256), ("r_k", 256), ("r_v", 512),
                  ("r_g", 512), ("cv_a", 512), ("cv_g", 512), ("gt", N_BRANCH * D_MODEL)):
    _OFF[_name] = (_o, _o + _w)
    _o += _w
D_IN = _o

V7X_VMEM_LIMIT = 56 * 1024 * 1024


def _params(semantics, vmem=V7X_VMEM_LIMIT):
    return pltpu.CompilerParams(dimension_semantics=semantics, vmem_limit_bytes=vmem)


def _resident(shape, index_map):
    return pl.BlockSpec(shape, index_map, pipeline_mode=pl.Buffered(1))


def _sigmoid(x):
    return 1.0 / (1.0 + jnp.exp(-x))


def _rms(x, g):
    return x * lax.rsqrt(jnp.mean(x * x, axis=-1, keepdims=True) + NORM_EPS) * g


def _modulated(x, g_pre, mod_ref, k0):
    shift = mod_ref[0, k0:k0 + 1, :]
    scale = mod_ref[0, k0 + 1:k0 + 2, :]
    return _rms(x, g_pre) * (1.0 + scale) + shift


def _ada_kernel(c_ref, w_ref, b_ref, o_ref):
    c = c_ref[...]
    sc = c * _sigmoid(c)
    o_ref[0] = jnp.dot(sc, w_ref[0], preferred_element_type=F32) + b_ref[0]


def _ada(cvec, ada_w, ada_b):
    depth, d, n = ada_w.shape
    tn = 1024
    return pl.pallas_call(
        _ada_kernel,
        grid=(depth, n // tn),
        in_specs=[pl.BlockSpec((8, d), lambda l, j: (0, 0)),
                  pl.BlockSpec((1, d, tn), lambda l, j: (l, 0, j)),
                  pl.BlockSpec((1, 1, tn), lambda l, j: (l, 0, j))],
        out_specs=pl.BlockSpec((1, 8, tn), lambda l, j: (l, 0, j)),
        out_shape=jax.ShapeDtypeStruct((depth, 8, n), F32),
        name="ada",
        compiler_params=_params(("parallel", "parallel")),
    )(cvec, ada_w, ada_b.reshape(depth, 1, n))


def _ffn_kernel(x_ref, mod_ref, gpre_ref, gpost_ref, wup_ref, wdn_ref, o_ref, *, k0):
    x = x_ref[0]
    h = _modulated(x, gpre_ref[...], mod_ref, k0).astype(BF16)
    ab = jnp.dot(h, wup_ref[...], preferred_element_type=F32)
    a = ab[:, :D_FF]
    b = ab[:, D_FF:]
    u = (a * _sigmoid(a) * b).astype(BF16)
    y = jnp.dot(u, wdn_ref[...], preferred_element_type=F32)
    gate = mod_ref[0, k0 + 2:k0 + 3, :]
    o_ref[0] = x + 0.5 * gate * _rms(y, gpost_ref[...])


def _ffn(x, mods, mod_row, g_pre, g_post, w_up, w_dn, layer, k0):
    b, n, d = x.shape
    tm = min(256, n)
    row = (lambda bi: bi) if mod_row is None else (lambda bi: mod_row)
    return pl.pallas_call(
        functools.partial(_ffn_kernel, k0=k0),
        grid=(b, n // tm),
        in_specs=[pl.BlockSpec((1, tm, d), lambda bi, i: (bi, i, 0)),
                  pl.BlockSpec((1, N_ADA, d), lambda bi, i: (row(bi), 0, 0)),
                  pl.BlockSpec((1, d), lambda bi, i: (0, 0)),
                  pl.BlockSpec((1, d), lambda bi, i: (0, 0)),
                  _resident((None, d, 2 * D_FF), lambda bi, i: (layer, 0, 0)),
                  _resident((None, D_FF, d), lambda bi, i: (layer, 0, 0))],
        out_specs=pl.BlockSpec((1, tm, d), lambda bi, i: (bi, i, 0)),
        out_shape=jax.ShapeDtypeStruct(x.shape, F32),
        name="ffn",
        compiler_params=_params(("parallel", "parallel")),
    )(x, mods, g_pre, g_post, w_up, w_dn)


def _swap16(c):
    lane = lax.broadcasted_iota(jnp.int32, c.shape, 1)
    return jnp.where((lane & 16) != 0, pltpu.roll(c, 16, 1), pltpu.roll(c, 112, 1))


def _inproj_kernel(*refs, rope):
    if rope:
        x_ref, mod_ref, gpre_ref, w_ref, cos_ref, sin_ref = refs[:6]
        outs = refs[6:]
    else:
        x_ref, mod_ref, gpre_ref, w_ref = refs[:4]
        outs = refs[6:]
    daq_ref, dak_ref, dav_ref, rq_ref, rk_ref, rv_ref, rg_ref, cu_ref, gt_ref = outs
    h = _modulated(x_ref[0], gpre_ref[...], mod_ref, 3).astype(BF16)
    y = jnp.dot(h, w_ref[...], preferred_element_type=F32)

    def rotary(name, out_ref, mul, transposed=False):
        lo, hi = _OFF[name]
        for j in range((hi - lo) // 128):
            c = y[:, lo + 128 * j:lo + 128 * (j + 1)]
            if rope:
                c = c * cos_ref[...] + _swap16(c) * sin_ref[...]
            if mul != 1.0:
                c = c * mul
            if transposed:
                out_ref[0, 0, 128 * j:128 * (j + 1), :] = c.T.astype(BF16)
            else:
                out_ref[0, :, 128 * j:128 * (j + 1)] = c.astype(BF16)

    def plain(name, out_ref):
        lo, hi = _OFF[name]
        out_ref[0] = y[:, lo:hi].astype(BF16)

    rotary("da_q", daq_ref, ATTN_Q_SCALE, transposed=True)
    rotary("da_k", dak_ref, 1.0)
    lo, hi = _OFF["da_v"]
    for j in range((hi - lo) // 128):
        dav_ref[0, 0, 128 * j:128 * (j + 1), :] = y[:, lo + 128 * j:lo + 128 * (j + 1)].T.astype(BF16)
    rotary("r_q", rq_ref, ROPE_DIM ** -0.5)
    rotary("r_k", rk_ref, 1.0)
    plain("r_v", rv_ref)
    plain("r_g", rg_ref)
    a = y[:, _OFF["cv_a"][0]:_OFF["cv_a"][1]]
    g = y[:, _OFF["cv_g"][0]:_OFF["cv_g"][1]]
    cu_ref[0] = (a * _sigmoid(g)).astype(BF16)
    plain("gt", gt_ref)


_INPROJ_WIDTHS = (512, 512, 512, 256, 256, 512, 512, 512, N_BRANCH * D_MODEL)


def _inproj(x, mods, mod_row, g_pre, w_in, layer, tables=None, append_to=None, extra_blocks=0):
    b, n, d = x.shape
    tm = ATTN_BLOCK
    nb = n // tm
    rope = tables is not None
    if append_to is None:
        kv_blocks, kv_off = nb + extra_blocks, 0
    else:
        kv_blocks = append_to[1].shape[1]
        kv_off = kv_blocks - nb
    token_major = lambda w: (pl.BlockSpec((1, tm, w), lambda bi, i: (bi, i, 0)), jax.ShapeDtypeStruct((b, n, w), BF16))
    outs = [(pl.BlockSpec((1, 1, 512, tm), lambda bi, i: (bi, i, 0, 0)), jax.ShapeDtypeStruct((b, nb, 512, tm), BF16)),
            (pl.BlockSpec((1, tm, 512), lambda bi, i: (bi, i + kv_off, 0)),
             jax.ShapeDtypeStruct((b, kv_blocks * tm, 512), BF16)),
            (pl.BlockSpec((1, 1, 512, tm), lambda bi, i: (bi, i + kv_off, 0, 0)),
             jax.ShapeDtypeStruct((b, kv_blocks, 512, tm), BF16))]
    outs += [token_major(w) for w in _INPROJ_WIDTHS[3:]]
    row = (lambda bi: bi) if mod_row is None else (lambda bi: mod_row)
    in_specs = [pl.BlockSpec((1, tm, d), lambda bi, i: (bi, i, 0)),
                pl.BlockSpec((1, N_ADA, d), lambda bi, i: (row(bi), 0, 0)),
                pl.BlockSpec((1, d), lambda bi, i: (0, 0)),
                _resident((None, d, D_IN), lambda bi, i: (layer, 0, 0))]
    args = [x, mods, g_pre, w_in]
    aliases = {}
    if rope:
        in_specs += [pl.BlockSpec((tm, 128), lambda bi, i: (i, 0))] * 2
        args += list(tables)
    else:
        in_specs += [pl.BlockSpec(memory_space=pl.ANY)] * 2
        args += list(append_to)
        aliases = {4: 1, 5: 2}
    return pl.pallas_call(
        functools.partial(_inproj_kernel, rope=rope),
        grid=(b, nb),
        in_specs=in_specs,
        out_specs=[o[0] for o in outs],
        out_shape=[o[1] for o in outs],
        input_output_aliases=aliases,
        name="inproj",
        compiler_params=_params(("parallel", "parallel")),
    )(*args)


def _rope_tables(n):
    t = jnp.arange(n, dtype=jnp.int32)
    row = (t // GRID_W).astype(F32)
    col = (t % GRID_W).astype(F32)
    axis_dim = ROPE_DIM // 2
    inv_freq = ROPE_BASE ** (-jnp.arange(0, axis_dim, 2, dtype=F32) / axis_dim)
    ang_r = row[:, None] * inv_freq
    ang_c = col[:, None] * inv_freq
    cos64 = jnp.concatenate([jnp.cos(ang_r), jnp.cos(ang_r), jnp.cos(ang_c), jnp.cos(ang_c)], axis=-1)
    sin64 = jnp.concatenate([-jnp.sin(ang_r), jnp.sin(ang_r), -jnp.sin(ang_c), jnp.sin(ang_c)], axis=-1)
    return jnp.tile(cos64, (1, 2)), jnp.tile(sin64, (1, 2))


def _attn_kernel(lam_ref, subln_ref, q_ref, k_ref, vt_ref, o_ref,
                 s_ref, p_ref, mb_ref, m_ref, l_ref, a_ref, acc_ref, *, n_blocks, lam_init):
    qt = q_ref[0, 0]
    row = lax.broadcasted_iota(jnp.int32, qt.shape, 0)
    zero = jnp.zeros_like(qt)
    qmaps = (jnp.where(row < ROPE_DIM, qt, zero), jnp.where(row >= ROPE_DIM, qt, zero))
    tk = ATTN_BLOCK

    def scores(j, slot):
        k = k_ref[0, pl.ds(pl.multiple_of(j * tk, tk), tk), :]
        for c in range(2):
            s = jnp.dot(k, qmaps[c], preferred_element_type=F32)
            s_ref[slot, c] = s
            mb_ref[slot, c] = jnp.max(s, axis=0, keepdims=True)

    def softmax(slot, first=False):
        for c in range(2):
            if first:
                m_new = mb_ref[slot, c]
                a_ref[c] = jnp.ones_like(m_new)
            else:
                m_old = m_ref[c]
                m_new = jnp.maximum(m_old, mb_ref[slot, c])
                alpha = jnp.exp2(m_old - m_new)
                a_ref[c] = alpha
            p = jnp.exp2(s_ref[slot, c] - m_new)
            p_sum = jnp.sum(p, axis=0, keepdims=True)
            l_ref[c] = p_sum if first else alpha * l_ref[c] + p_sum
            m_ref[c] = m_new
            p_ref[slot, c] = p.astype(BF16)

    def values(j, slot):
        vt = vt_ref[0, j]
        for c in range(2):
            pv = jnp.dot(vt, p_ref[slot, c], preferred_element_type=F32)
            acc_ref[c] = a_ref[c] * acc_ref[c] + pv

    acc_ref[...] = jnp.zeros_like(acc_ref)
    scores(0, 0)
    if n_blocks > 1:
        scores(1, 1)
    softmax(0, first=True)
    def step(j, cur):
        scores(j + 1, 1 - cur)
        values(j - 1, 1 - cur)
        softmax(cur)

    steady = max(n_blocks - 2, 0)
    if steady >= 2:
        def body(i, carry):
            step(2 * i + 1, 1)
            step(2 * i + 2, 0)
            return carry

        lax.fori_loop(0, steady // 2, body, 0)
    if steady % 2:
        step(n_blocks - 2, (n_blocks - 2) % 2)
    if n_blocks > 1:
        values(n_blocks - 2, n_blocks % 2)
        softmax((n_blocks - 1) % 2)
    values(n_blocks - 1, (n_blocks - 1) % 2)

    lv = lam_ref[...]
    lam = (jnp.exp(jnp.sum(lv[0:1] * lv[1:2], axis=-1, keepdims=True))
           - jnp.exp(jnp.sum(lv[2:3] * lv[3:4], axis=-1, keepdims=True)) + lam_init)
    ot = acc_ref[0] / l_ref[0] - lam * (acc_ref[1] / l_ref[1])
    o_ref[0] = (_rms(ot.T, subln_ref[...]) * (1.0 - lam_init)).astype(BF16)


def _attention(q, k, vt, first_block, n_blocks, da_lambda, subln, lam_init):
    b, nqb, _, tq = q.shape
    hl = HEAD_LANES
    tk = ATTN_BLOCK
    assert first_block % n_blocks == 0
    in_specs = [pl.BlockSpec((4, ROPE_DIM), lambda bi, h, i: (0, 0)),
                pl.BlockSpec((1, hl), lambda bi, h, i: (0, 0)),
                pl.BlockSpec((1, 1, hl, tq), lambda bi, h, i: (bi, i, h, 0)),
                pl.BlockSpec((1, n_blocks * tk, hl), lambda bi, h, i: (bi, first_block // n_blocks, h)),
                pl.BlockSpec((1, n_blocks, hl, tk), lambda bi, h, i: (bi, first_block // n_blocks, h, 0))]
    args = [da_lambda, subln, q, k, vt]
    return pl.pallas_call(
        functools.partial(_attn_kernel, n_blocks=n_blocks, lam_init=lam_init),
        grid=(b, HEADS, nqb),
        in_specs=in_specs,
        out_specs=pl.BlockSpec((1, tq, hl), lambda bi, h, i: (bi, i, h)),
        out_shape=jax.ShapeDtypeStruct((b, nqb * tq, MIX_W), BF16),
        scratch_shapes=[pltpu.VMEM((2, 2, tk, tq), F32),
                        pltpu.VMEM((2, 2, tk, tq), BF16),
                        pltpu.VMEM((2, 2, 1, tq), F32),
                        pltpu.VMEM((2, 1, tq), F32),
                        pltpu.VMEM((2, 1, tq), F32),
                        pltpu.VMEM((2, 1, tq), F32),
                        pltpu.VMEM((2, hl, tq), F32)],
        name="attn",
        compiler_params=_params(("parallel", "parallel", "arbitrary")),
    )(*args)


def _ret_kernel(*refs, reverse, chunks):
    if reverse:
        (dec_ref, q_ref, k_ref, v_ref, s0_ref, of_ref, g_ref, gng_ref, gnb_ref,
         o_ref, sfin_ref, state_ref) = refs
    else:
        dec_ref, q_ref, k_ref, v_ref, s0_ref, o_ref, sfin_ref, state_ref = refs
    step = pl.program_id(1)

    @pl.when(step == 0)
    def _():
        state_ref[...] = s0_ref[0]

    cch = RET_CHUNK
    ii = lax.broadcasted_iota(jnp.int32, (cch, cch), 0).astype(F32)
    ss = lax.broadcasted_iota(jnp.int32, (cch, cch), 1).astype(F32)
    pos = lax.broadcasted_iota(jnp.int32, (cch, 1), 0).astype(F32)
    lane = lax.broadcasted_iota(jnp.int32, (cch, 2 * ROPE_DIM), 1)
    d_row = 1 if reverse else 0
    for h in range(HEADS):
        x = dec_ref[d_row:d_row + 1, h:h + 1]
        lg = jnp.minimum(x, 0.0) - jnp.log(1.0 + jnp.exp(-jnp.abs(x)))
        if reverse:
            dist = ss - ii
            dmat = jnp.where(dist > 0, jnp.exp(lg * jnp.maximum(dist, 0.0)), 0.0)
            zeta = jnp.exp(lg * pos)
            xi = jnp.exp(lg * (cch - pos))
        else:
            dist = ii - ss
            dmat = jnp.where(dist >= 0, jnp.exp(lg * jnp.maximum(dist, 0.0)), 0.0)
            zeta = jnp.exp(lg * (cch - 1 - pos))
            xi = jnp.exp(lg * (pos + 1.0))
        chunk_decay = jnp.exp(lg * cch)
        pair = slice(128 * (h // 2), 128 * (h // 2 + 1))
        vcol = slice(128 * h, 128 * (h + 1))
        lo = ROPE_DIM * (h % 2)
        head_lanes = (lane >= lo) & (lane < lo + ROPE_DIM)
        order = range(chunks - 1, -1, -1) if reverse else range(chunks)
        for c in order:
            rows = slice(cch * c, cch * (c + 1))
            qc = q_ref[0, rows, pair]
            kc = k_ref[0, rows, pair]
            kc = jnp.where(head_lanes, kc, jnp.zeros_like(kc))
            vc = v_ref[0, rows, vcol]
            sc = lax.dot_general(qc, kc, (((1,), (1,)), ((), ())), preferred_element_type=F32) * dmat
            o = jnp.dot(sc.astype(BF16), vc, preferred_element_type=F32)
            state = state_ref[h]
            o = o + xi * jnp.dot(qc, state.astype(BF16), preferred_element_type=F32)
            vz = (vc.astype(F32) * zeta).astype(BF16)
            kv = lax.dot_general(kc, vz, (((0,), (0,)), ((), ())), preferred_element_type=F32)
            state_ref[h] = chunk_decay * state + kv
            if reverse:
                t = o + of_ref[0, rows, vcol]
                mu = jnp.mean(t, axis=-1, keepdims=True)
                tc = t - mu
                var = jnp.mean(tc * tc, axis=-1, keepdims=True)
                y = tc * lax.rsqrt(var + NORM_EPS) * gng_ref[:, vcol] + gnb_ref[:, vcol]
                g = g_ref[0, rows, vcol].astype(F32)
                o_ref[0, rows, vcol] = (y * (g * _sigmoid(g))).astype(BF16)
            else:
                o_ref[0, rows, vcol] = o

    @pl.when(step == pl.num_programs(1) - 1)
    def _():
        sfin_ref[0] = state_ref[...]


def _retention_pass(decay, q, k, v, s0, post=None):
    b, n, _ = q.shape
    t = next(c for c in (512, 256, RET_CHUNK) if n % c == 0)
    steps = n // t
    reverse = post is not None
    blk = (lambda bi, i: (bi, steps - 1 - i, 0)) if reverse else (lambda bi, i: (bi, i, 0))
    const2 = lambda bi, i: (0, 0)
    state_spec = pl.BlockSpec((1, HEADS, 128, 128), lambda bi, i: (bi, 0, 0, 0))
    in_specs = [pl.BlockSpec((2, HEADS), const2),
                pl.BlockSpec((1, t, 256), blk), pl.BlockSpec((1, t, 256), blk), pl.BlockSpec((1, t, MIX_W), blk),
                state_spec]
    args = [decay, q, k, v, s0]
    if reverse:
        o_fwd, gate, gn_g, gn_b = post
        in_specs += [pl.BlockSpec((1, t, MIX_W), blk), pl.BlockSpec((1, t, MIX_W), blk),
                     pl.BlockSpec((1, MIX_W), const2), pl.BlockSpec((1, MIX_W), const2)]
        args += [o_fwd, gate, gn_g, gn_b]
    return pl.pallas_call(
        functools.partial(_ret_kernel, reverse=reverse, chunks=t // RET_CHUNK),
        grid=(b, steps),
        in_specs=in_specs,
        out_specs=[pl.BlockSpec((1, t, MIX_W), blk), state_spec],
        out_shape=[jax.ShapeDtypeStruct((b, n, MIX_W), BF16 if reverse else F32),
                   jax.ShapeDtypeStruct((b, HEADS, 128, 128), F32)],
        scratch_shapes=[pltpu.VMEM((HEADS, 128, 128), F32)],
        name="retention_bwd" if reverse else "retention_fwd",
        compiler_params=_params(("parallel", "arbitrary")),
    )(*args)


def _conv_kernel(prev_ref, cur_ref, next_ref, dw_ref, dwb_ref, lng_ref, lnb_ref, o_ref, buf_ref, *, t):
    i = pl.program_id(1)
    halo = CONV_HALO
    first = i == 0
    last = i == pl.num_programs(1) - 1
    prev = prev_ref[0].astype(F32)
    nxt = next_ref[0].astype(F32)
    buf_ref[0:halo, :] = jnp.where(first, jnp.zeros_like(prev), prev)
    buf_ref[halo:halo + t, :] = cur_ref[0].astype(F32)
    buf_ref[halo + t:2 * halo + t, :] = jnp.where(last, jnp.zeros_like(nxt), nxt)
    rb = 32
    off = halo - CONV_WIDTH // 2
    for r in range(t // rb):
        acc = jnp.broadcast_to(dwb_ref[...], (rb, CONV_CH))
        for w in range(CONV_WIDTH):
            acc = acc + buf_ref[rb * r + off + w:rb * r + off + w + rb, :] * dw_ref[w:w + 1, :]
        mu = jnp.mean(acc, axis=-1, keepdims=True)
        ac = acc - mu
        var = jnp.mean(ac * ac, axis=-1, keepdims=True)
        y = ac * lax.rsqrt(var + NORM_EPS) * lng_ref[...] + lnb_ref[...]
        o_ref[0, rb * r:rb * (r + 1), :] = (y * _sigmoid(y)).astype(BF16)


def _conv(u, dw, dw_b, ln_g, ln_b):
    b, n, ch = u.shape
    t = min(256, n)
    hb = t // CONV_HALO
    nh = n // CONV_HALO
    const2 = lambda bi, i: (0, 0)
    return pl.pallas_call(
        functools.partial(_conv_kernel, t=t),
        grid=(b, n // t),
        in_specs=[pl.BlockSpec((1, CONV_HALO, ch), lambda bi, i: (bi, jnp.maximum(i * hb - 1, 0), 0)),
                  pl.BlockSpec((1, t, ch), lambda bi, i: (bi, i, 0)),
                  pl.BlockSpec((1, CONV_HALO, ch), lambda bi, i: (bi, jnp.minimum((i + 1) * hb, nh - 1), 0)),
                  pl.BlockSpec((CONV_WIDTH, ch), const2),
                  pl.BlockSpec((1, ch), const2), pl.BlockSpec((1, ch), const2), pl.BlockSpec((1, ch), const2)],
        out_specs=pl.BlockSpec((1, t, ch), lambda bi, i: (bi, i, 0)),
        out_shape=jax.ShapeDtypeStruct((b, n, ch), BF16),
        scratch_shapes=[pltpu.VMEM((t + 2 * CONV_HALO, ch), F32)],
        name="conv",
        compiler_params=_params(("parallel", "parallel")),
    )(u, u, u, dw, dw_b, ln_g, ln_b)


def _merge_kernel(x_ref, mod_ref, gpost_ref, oa_ref, yr_ref, yc_ref, gt_ref,
                  wa_ref, wr_ref, wc_ref, wo_ref, o_ref):
    d = D_MODEL
    m = None
    for j, (y_ref, w_ref) in enumerate(((oa_ref, wa_ref), (yr_ref, wr_ref), (yc_ref, wc_ref))):
        y = jnp.dot(y_ref[0], w_ref[...], preferred_element_type=F32)
        term = _sigmoid(gt_ref[0, :, d * j:d * (j + 1)].astype(F32)) * y
        m = term if m is None else m + term
    y = jnp.dot(m.astype(BF16), wo_ref[...], preferred_element_type=F32)
    gate = mod_ref[0, 5:6, :]
    o_ref[0] = x_ref[0] + gate * _rms(y, gpost_ref[...])


def _merge(x, mods, mod_row, g_post, oa, yr, yc, gt, w_a, w_r, w_c, w_o, layer):
    b, n, d = x.shape
    tm = min(256, n)
    row = (lambda bi: bi) if mod_row is None else (lambda bi: mod_row)
    tok = lambda w: pl.BlockSpec((1, tm, w), lambda bi, i: (bi, i, 0))
    wsp = lambda k: _resident((None, k, d), lambda bi, i: (layer, 0, 0))
    return pl.pallas_call(
        _merge_kernel,
        grid=(b, n // tm),
        in_specs=[tok(d), pl.BlockSpec((1, N_ADA, d), lambda bi, i: (row(bi), 0, 0)),
                  pl.BlockSpec((1, d), lambda bi, i: (0, 0)),
                  tok(MIX_W), tok(MIX_W), tok(MIX_W), tok(N_BRANCH * d),
                  wsp(MIX_W), wsp(MIX_W), wsp(MIX_W), wsp(d)],
        out_specs=tok(d),
        out_shape=jax.ShapeDtypeStruct(x.shape, F32),
        name="merge",
        compiler_params=_params(("parallel", "parallel")),
    )(x, mods, g_post, oa, yr, yc, gt, w_a, w_r, w_c, w_o)


def kernel(x, c, ctx, c_ctx, ada_w, ada_b, norm_pre, norm_post, ffn1_up, ffn1_down, ffn2_up, ffn2_down, w_in, da_lambda, da_subln, da_proj, ret_decay, ret_gn_g, ret_gn_b, ret_proj, conv_dw, conv_dw_b, conv_ln_g, conv_ln_b, conv_proj, w_out):
    depth = ada_w.shape[0]
    b, n, d = x.shape
    ctx_row = b
    cvec = jnp.zeros((8, d), F32).at[:b].set(c).at[ctx_row].set(c_ctx)
    mods = _ada(cvec, ada_w, ada_b).reshape(depth, 8, N_ADA, d)
    tables = _rope_tables(n)
    wb = lambda w: w.astype(BF16)
    ffn1_up, ffn1_down, ffn2_up, ffn2_down = wb(ffn1_up), wb(ffn1_down), wb(ffn2_up), wb(ffn2_down)
    w_in, da_proj, ret_proj, conv_proj, w_out = wb(w_in), wb(da_proj), wb(ret_proj), wb(conv_proj), wb(w_out)
    zero_state = jnp.zeros((b, HEADS, 128, 128), F32)

    for l in range(depth):
        lam_init = 0.8 - 0.6 * math.exp(-0.3 * l)
        mod_l = mods[l]
        row = lambda a, j: a[l, j][None, :]
        ctx = _ffn(ctx, mod_l, ctx_row, row(norm_pre, 0), row(norm_post, 0), ffn1_up, ffn1_down, l, 0)
        x = _ffn(x, mod_l, None, row(norm_pre, 0), row(norm_post, 0), ffn1_up, ffn1_down, l, 0)
        ctx_blocks = ctx.shape[1] // ATTN_BLOCK
        x_blocks = n // ATTN_BLOCK
        px = _inproj(x, mod_l, None, row(norm_pre, 1), w_in, l, tables=tables, extra_blocks=ctx_blocks)
        pc = _inproj(ctx, mod_l, ctx_row, row(norm_pre, 1), w_in, l, append_to=(px[1], px[2]))
        k_all, vt_all = pc[1], pc[2]
        subln = da_subln[l][None, :]
        dec = ret_decay[l]
        gn_g, gn_b = ret_gn_g[l][None, :], ret_gn_b[l][None, :]
        conv_args = (conv_dw[l], conv_dw_b[l][None, :], conv_ln_g[l][None, :], conv_ln_b[l][None, :])
        oa_c = _attention(pc[0], k_all, vt_all, x_blocks, ctx_blocks, da_lambda[l], subln, lam_init)
        of_c, sf_c = _retention_pass(dec, pc[3], pc[4], pc[5], zero_state)
        yr_c, sb_c = _retention_pass(dec, pc[3], pc[4], pc[5], zero_state, post=(of_c, pc[6], gn_g, gn_b))
        yc_c = _conv(pc[7], *conv_args)
        oa_x = _attention(px[0], k_all, vt_all, 0, x_blocks + ctx_blocks, da_lambda[l], subln, lam_init)
        of_x, _ = _retention_pass(dec, px[3], px[4], px[5], sf_c)
        yr_x, _ = _retention_pass(dec, px[3], px[4], px[5], sb_c, post=(of_x, px[6], gn_g, gn_b))
        yc_x = _conv(px[7], *conv_args)
        outs = []
        for tok, mrow, oa, yr, yc, gt in ((ctx, ctx_row, oa_c, yr_c, yc_c, pc[8]), (x, None, oa_x, yr_x, yc_x, px[8])):
            tok = _merge(tok, mod_l, mrow, row(norm_post, 1), oa, yr, yc, gt, da_proj, ret_proj, conv_proj, w_out, l)
            tok = _ffn(tok, mod_l, mrow, row(norm_pre, 2), row(norm_post, 2), ffn2_up, ffn2_down, l, 6)
            outs.append(tok)
        ctx, x = outs
    return x
```

```python
import functools
import math

import jax
import jax.numpy as jnp
from jax import lax
from jax.experimental import pallas as pl
from jax.experimental.pallas import tpu as pltpu

F32 = jnp.float32
BF16 = jnp.bfloat16

D_MODEL = 1024
N_ADA = 9
GRID_W = 64
ROPE_DIM = 64
ROPE_BASE = 10000.0
NORM_EPS = 1e-6
HEADS = 4
HEAD_LANES = 128
V_ROWS = HEAD_LANES + 16
ATTN_BLOCK = 256
ATTN_GROUP = 512
ATTN_LEAD = 2
ATTN_SLOTS = ATTN_LEAD + 2
ATTN_STEPS_PER_TRIP = 12
ATTN_Q_SCALE = ROPE_DIM ** -0.5 * math.log2(math.e)
RET_CHUNK = 128
CONV_CH = 512
CONV_WIDTH = 31
CONV_HALO = 16
F32_SUBLANES = 8
D_FF = 2816
MIX_W = 512
N_BRANCH = 3
_OFF = {}
_o = 0
for _name, _w in (("da_q", 512), ("da_k", 512), ("da_v", 512), ("r_q", 256), ("r_k", 256), ("r_v", 512),
                  ("r_g", 512), ("cv_a", 512), ("cv_g", 512), ("gt", N_BRANCH * D_MODEL)):
    _OFF[_name] = (_o, _o + _w)
    _o += _w
D_IN = _o

V7X_VMEM_LIMIT = 56 * 1024 * 1024


def _params(semantics, vmem=V7X_VMEM_LIMIT):
    return pltpu.CompilerParams(dimension_semantics=semantics, vmem_limit_bytes=vmem)


def _resident(shape, index_map):
    return pl.BlockSpec(shape, index_map, pipeline_mode=pl.Buffered(1))


def _sigmoid(x):
    return 1.0 / (1.0 + jnp.exp(-x))


def _rms(x, g):
    return x * lax.rsqrt(jnp.mean(x * x, axis=-1, keepdims=True) + NORM_EPS) * g


def _modulated(x, g_pre, mod_ref, k0):
    shift = mod_ref[0, k0:k0 + 1, :]
    scale = mod_ref[0, k0 + 1:k0 + 2, :]
    return _rms(x, g_pre) * (1.0 + scale) + shift


def _ada_kernel(c_ref, w_ref, b_ref, o_ref):
    c = c_ref[...]
    sc = c * _sigmoid(c)
    o_ref[0] = jnp.dot(sc, w_ref[0], preferred_element_type=F32) + b_ref[0]


def _ada(cvec, ada_w, ada_b):
    depth, d, n = ada_w.shape
    tn = 1024
    return pl.pallas_call(
        _ada_kernel,
        grid=(depth, n // tn),
        in_specs=[pl.BlockSpec((8, d), lambda l, j: (0, 0)),
                  pl.BlockSpec((1, d, tn), lambda l, j: (l, 0, j)),
                  pl.BlockSpec((1, 1, tn), lambda l, j: (l, 0, j))],
        out_specs=pl.BlockSpec((1, 8, tn), lambda l, j: (l, 0, j)),
        out_shape=jax.ShapeDtypeStruct((depth, 8, n), F32),
        name="ada",
        compiler_params=_params(("parallel", "parallel")),
    )(cvec, ada_w, ada_b.reshape(depth, 1, n))


def _ffn_kernel(x_ref, mod_ref, gpre_ref, gpost_ref, wup_ref, wdn_ref, o_ref, *, k0):
    x = x_ref[0]
    h = _modulated(x, gpre_ref[...], mod_ref, k0).astype(BF16)
    ab = jnp.dot(h, wup_ref[...], preferred_element_type=F32)
    a = ab[:, :D_FF]
    b = ab[:, D_FF:]
    u = (a * _sigmoid(a) * b).astype(BF16)
    y = jnp.dot(u, wdn_ref[...], preferred_element_type=F32)
    gate = mod_ref[0, k0 + 2:k0 + 3, :]
    o_ref[0] = x + 0.5 * gate * _rms(y, gpost_ref[...])


def _ffn(x, mods, mod_row, g_pre, g_post, w_up, w_dn, layer, k0):
    b, n, d = x.shape
    tm = min(256, n)
    row = (lambda bi: bi) if mod_row is None else (lambda bi: mod_row)
    return pl.pallas_call(
        functools.partial(_ffn_kernel, k0=k0),
        grid=(b, n // tm),
        in_specs=[pl.BlockSpec((1, tm, d), lambda bi, i: (bi, i, 0)),
                  pl.BlockSpec((1, N_ADA, d), lambda bi, i: (row(bi), 0, 0)),
                  pl.BlockSpec((1, d), lambda bi, i: (0, 0)),
                  pl.BlockSpec((1, d), lambda bi, i: (0, 0)),
                  _resident((None, d, 2 * D_FF), lambda bi, i: (layer, 0, 0)),
                  _resident((None, D_FF, d), lambda bi, i: (layer, 0, 0))],
        out_specs=pl.BlockSpec((1, tm, d), lambda bi, i: (bi, i, 0)),
        out_shape=jax.ShapeDtypeStruct(x.shape, F32),
        name="ffn",
        compiler_params=_params(("parallel", "parallel")),
    )(x, mods, g_pre, g_post, w_up, w_dn)


def _swap16(c):
    lane = lax.broadcasted_iota(jnp.int32, c.shape, 1)
    return jnp.where((lane & 16) != 0, pltpu.roll(c, 16, 1), pltpu.roll(c, 112, 1))


def _inproj_kernel(*refs, rope):
    if rope:
        x_ref, mod_ref, gpre_ref, w_ref, cos_ref, sin_ref = refs[:6]
        outs = refs[6:]
    else:
        x_ref, mod_ref, gpre_ref, w_ref = refs[:4]
        outs = refs[6:]
    daq_ref, dak_ref, dav_ref, rq_ref, rk_ref, rv_ref, rg_ref, cu_ref, gt_ref = outs
    h = _modulated(x_ref[0], gpre_ref[...], mod_ref, 3).astype(BF16)
    y = jnp.dot(h, w_ref[...], preferred_element_type=F32)

    def rotary(name, out_ref, mul, transposed=False):
        lo, hi = _OFF[name]
        for j in range((hi - lo) // 128):
            c = y[:, lo + 128 * j:lo + 128 * (j + 1)]
            if rope:
                c = c * cos_ref[...] + _swap16(c) * sin_ref[...]
            if mul != 1.0:
                c = c * mul
            if transposed:
                out_ref[0, 128 * j:128 * (j + 1), :] = c.T.astype(BF16)
            else:
                out_ref[0, :, 128 * j:128 * (j + 1)] = c.astype(BF16)

    def plain(name, out_ref):
        lo, hi = _OFF[name]
        out_ref[0] = y[:, lo:hi].astype(BF16)

    rotary("da_q", daq_ref, ATTN_Q_SCALE, transposed=True)
    rotary("da_k", dak_ref, 1.0)
    lo, hi = _OFF["da_v"]
    for j in range((hi - lo) // 128):
        base = V_ROWS * j
        dav_ref[0, base:base + 128, :] = y[:, lo + 128 * j:lo + 128 * (j + 1)].T.astype(BF16)
        pad = lax.broadcasted_iota(jnp.int32, (V_ROWS - 128, y.shape[0]), 0)
        dav_ref[0, base + 128:base + V_ROWS, :] = jnp.where(pad == 0, 1.0, 0.0).astype(BF16)
    rotary("r_q", rq_ref, ROPE_DIM ** -0.5)
    rotary("r_k", rk_ref, 1.0)
    plain("r_v", rv_ref)
    plain("r_g", rg_ref)
    a = y[:, _OFF["cv_a"][0]:_OFF["cv_a"][1]]
    g = y[:, _OFF["cv_g"][0]:_OFF["cv_g"][1]]
    cu_ref[0] = (a * _sigmoid(g)).astype(BF16)
    plain("gt", gt_ref)


_INPROJ_WIDTHS = (512, 512, 512, 256, 256, 512, 512, 512, N_BRANCH * D_MODEL)


def _inproj(x, mods, mod_row, g_pre, w_in, layer, tables=None, prepend_to=None, lead_blocks=0):
    b, n, d = x.shape
    tm = ATTN_BLOCK
    nb = n // tm
    rope = tables is not None
    if prepend_to is None:
        kv_blocks, kv_off = nb + lead_blocks, lead_blocks
    else:
        kv_blocks, kv_off = prepend_to[0].shape[1] // tm, 0
    token_major = lambda w: (pl.BlockSpec((1, tm, w), lambda bi, i: (bi, i, 0)), jax.ShapeDtypeStruct((b, n, w), BF16))
    outs = [(pl.BlockSpec((1, 512, tm), lambda bi, i: (bi, 0, i)), jax.ShapeDtypeStruct((b, 512, n), BF16)),
            (pl.BlockSpec((1, tm, 512), lambda bi, i: (bi, i + kv_off, 0)),
             jax.ShapeDtypeStruct((b, kv_blocks * tm, 512), BF16)),
            (pl.BlockSpec((1, HEADS * V_ROWS, tm), lambda bi, i: (bi, 0, i + kv_off)),
             jax.ShapeDtypeStruct((b, HEADS * V_ROWS, kv_blocks * tm), BF16))]
    outs += [token_major(w) for w in _INPROJ_WIDTHS[3:]]
    row = (lambda bi: bi) if mod_row is None else (lambda bi: mod_row)
    in_specs = [pl.BlockSpec((1, tm, d), lambda bi, i: (bi, i, 0)),
                pl.BlockSpec((1, N_ADA, d), lambda bi, i: (row(bi), 0, 0)),
                pl.BlockSpec((1, d), lambda bi, i: (0, 0)),
                _resident((None, d, D_IN), lambda bi, i: (layer, 0, 0))]
    args = [x, mods, g_pre, w_in]
    aliases = {}
    if rope:
        in_specs += [pl.BlockSpec((tm, 128), lambda bi, i: (i, 0))] * 2
        args += list(tables)
    else:
        in_specs += [pl.BlockSpec(memory_space=pl.ANY)] * 2
        args += list(prepend_to)
        aliases = {4: 1, 5: 2}
    return pl.pallas_call(
        functools.partial(_inproj_kernel, rope=rope),
        grid=(b, nb),
        in_specs=in_specs,
        out_specs=[o[0] for o in outs],
        out_shape=[o[1] for o in outs],
        input_output_aliases=aliases,
        name="inproj",
        compiler_params=_params(("parallel", "parallel")),
    )(*args)


def _rope_tables(n):
    t = jnp.arange(n, dtype=jnp.int32)
    row = (t // GRID_W).astype(F32)
    col = (t % GRID_W).astype(F32)
    axis_dim = ROPE_DIM // 2
    inv_freq = ROPE_BASE ** (-jnp.arange(0, axis_dim, 2, dtype=F32) / axis_dim)
    ang_r = row[:, None] * inv_freq
    ang_c = col[:, None] * inv_freq
    cos64 = jnp.concatenate([jnp.cos(ang_r), jnp.cos(ang_r), jnp.cos(ang_c), jnp.cos(ang_c)], axis=-1)
    sin64 = jnp.concatenate([-jnp.sin(ang_r), jnp.sin(ang_r), -jnp.sin(ang_c), jnp.sin(ang_c)], axis=-1)
    return jnp.tile(cos64, (1, 2)), jnp.tile(sin64, (1, 2))


def _attn_kernel(lam_ref, subln_ref, q_ref, k_ref, vt_ref, o_ref, *scratch, head, group, n_groups, lam_init):
    s_refs = scratch[:ATTN_SLOTS]
    p_refs = scratch[ATTN_SLOTS:2 * ATTN_SLOTS]
    m_ref, a_ref, mu_ref = scratch[2 * ATTN_SLOTS:2 * ATTN_SLOTS + 3]
    acc_refs = scratch[2 * ATTN_SLOTS + 3:]
    qt = q_ref[0]
    row = lax.broadcasted_iota(jnp.int32, qt.shape, 0)
    zero = jnp.zeros_like(qt)
    qmaps = (jnp.where(row < ROPE_DIM, qt, zero), jnp.where(row >= ROPE_DIM, qt, zero))
    row0 = pl.multiple_of(jnp.minimum(pl.program_id(2), 0), ATTN_GROUP)

    def rows(size):
        return pl.ds(row0, size)

    n_units = n_groups + 1

    def span(v):
        if isinstance(v, int):
            return (0, head) if v == 0 else (head + (v - 1) * group, group)
        return pl.multiple_of(head + (v - 1) * group, ATTN_BLOCK), group

    def scores(v, slot):
        start, size = span(v)
        k = k_ref[0, pl.ds(start, size), :]
        for c in range(2):
            s = jnp.dot(k, qmaps[c], preferred_element_type=F32)
            s_refs[slot][c, :size] = s
            mu_ref[slot, c] = jnp.max(s, axis=0, keepdims=True)

    def softmax(size, slot, first=False):
        for c in range(2):
            m_unit = mu_ref[slot, c]
            if first:
                m_new = m_unit
                a_ref[c] = jnp.ones_like(m_new)
            else:
                m_old = m_ref[c]
                m_new = jnp.maximum(m_old, m_unit)
                alpha = jnp.exp2(m_old - m_new)
                a_ref[c] = alpha
            m_ref[c] = m_new
            p_refs[slot][c, :size] = jnp.exp2(s_refs[slot][c, rows(size)] - m_new).astype(BF16)

    def values(v, slot):
        start, size = span(v)
        vt = vt_ref[0, :, pl.ds(start, size)]
        for c in range(2):
            pv = jnp.dot(vt, p_refs[slot][c, rows(size)], preferred_element_type=F32)
            acc_refs[c][...] = a_ref[c] * acc_refs[c][rows(V_ROWS)] + pv

    def step(v, cur):
        static = isinstance(v, int)
        if not static or v + ATTN_LEAD < n_units:
            scores(v + ATTN_LEAD, (cur + ATTN_LEAD) % ATTN_SLOTS)
        if not static or v >= 1:
            values(v - 1, (cur - 1) % ATTN_SLOTS)
        softmax(head if static and v == 0 else group, cur, first=static and v == 0)

    for acc_ref in acc_refs:
        acc_ref[...] = jnp.zeros_like(acc_ref)
    for v in range(min(ATTN_LEAD, n_units)):
        scores(v, v % ATTN_SLOTS)
    first_loop, end_loop = 2, n_units - ATTN_LEAD
    trips = max(end_loop - first_loop, 0) // ATTN_STEPS_PER_TRIP
    for v in range(min(first_loop, n_units)):
        step(v, v % ATTN_SLOTS)
    if trips:
        def body(i, carry):
            for t in range(ATTN_STEPS_PER_TRIP):
                step(first_loop + ATTN_STEPS_PER_TRIP * i + t, (first_loop + t) % ATTN_SLOTS)
            return carry

        lax.fori_loop(0, trips, body, 0)
    for v in range(first_loop + trips * ATTN_STEPS_PER_TRIP, n_units):
        step(v, v % ATTN_SLOTS)
    values(n_units - 1, (n_units - 1) % ATTN_SLOTS)

    lv = lam_ref[...]
    lam = (jnp.exp(jnp.sum(lv[0:1] * lv[1:2], axis=-1, keepdims=True))
           - jnp.exp(jnp.sum(lv[2:3] * lv[3:4], axis=-1, keepdims=True)) + lam_init)
    hl = HEAD_LANES
    normed = [acc_ref[:hl] / acc_ref[hl:hl + 1] for acc_ref in acc_refs]
    ot = normed[0] - lam * normed[1]
    o_ref[0] = (_rms(ot.T, subln_ref[...]) * (1.0 - lam_init)).astype(BF16)


def _attention(q, k, vt, n_keys, da_lambda, subln, lam_init):
    b, _, nq = q.shape
    hl = HEAD_LANES
    tq = ATTN_BLOCK
    head = ATTN_BLOCK
    rest = n_keys - head
    group = ATTN_GROUP if rest % ATTN_GROUP == 0 else ATTN_BLOCK
    rows = max(group, head)
    in_specs = [pl.BlockSpec((4, ROPE_DIM), lambda bi, h, i: (0, 0)),
                pl.BlockSpec((1, hl), lambda bi, h, i: (0, 0)),
                pl.BlockSpec((1, hl, tq), lambda bi, h, i: (bi, h, i)),
                pl.BlockSpec((1, n_keys, hl), lambda bi, h, i: (bi, 0, h)),
                pl.BlockSpec((1, V_ROWS, n_keys), lambda bi, h, i: (bi, h, 0))]
    args = [da_lambda, subln, q, k, vt]
    return pl.pallas_call(
        functools.partial(_attn_kernel, head=head, group=group, n_groups=rest // group, lam_init=lam_init),
        grid=(b, HEADS, nq // tq),
        in_specs=in_specs,
        out_specs=pl.BlockSpec((1, tq, hl), lambda bi, h, i: (bi, i, h)),
        out_shape=jax.ShapeDtypeStruct((b, nq, MIX_W), BF16),
        scratch_shapes=([pltpu.VMEM((2, rows, tq), F32)] * ATTN_SLOTS
                        + [pltpu.VMEM((2, rows, tq), BF16)] * ATTN_SLOTS
                        + [pltpu.VMEM((2, 1, tq), F32)] * 2
                        + [pltpu.VMEM((ATTN_SLOTS, 2, 1, tq), F32)]
                        + [pltpu.VMEM((V_ROWS, tq), F32)] * 2),
        name="attn",
        compiler_params=_params(("parallel", "parallel", "arbitrary")),
    )(*args)


def _ret_kernel(*refs, reverse, chunks):
    if reverse:
        (dec_ref, q_ref, k_ref, v_ref, s0_ref, of_ref, g_ref, gng_ref, gnb_ref,
         o_ref, sfin_ref, state_ref) = refs
    else:
        dec_ref, q_ref, k_ref, v_ref, s0_ref, o_ref, sfin_ref, state_ref = refs
    step = pl.program_id(1)

    @pl.when(step == 0)
    def _():
        state_ref[...] = s0_ref[0]

    cch = RET_CHUNK
    ii = lax.broadcasted_iota(jnp.int32, (cch, cch), 0).astype(F32)
    ss = lax.broadcasted_iota(jnp.int32, (cch, cch), 1).astype(F32)
    pos = lax.broadcasted_iota(jnp.int32, (cch, 1), 0).astype(F32)
    lane = lax.broadcasted_iota(jnp.int32, (cch, 2 * ROPE_DIM), 1)
    d_row = 1 if reverse else 0
    for h in range(HEADS):
        x = dec_ref[d_row:d_row + 1, h:h + 1]
        lg = jnp.minimum(x, 0.0) - jnp.log(1.0 + jnp.exp(-jnp.abs(x)))
        if reverse:
            dist = ss - ii
            dmat = jnp.where(dist > 0, jnp.exp(lg * jnp.maximum(dist, 0.0)), 0.0)
            zeta = jnp.exp(lg * pos)
            xi = jnp.exp(lg * (cch - pos))
        else:
            dist = ii - ss
            dmat = jnp.where(dist >= 0, jnp.exp(lg * jnp.maximum(dist, 0.0)), 0.0)
            zeta = jnp.exp(lg * (cch - 1 - pos))
            xi = jnp.exp(lg * (pos + 1.0))
        chunk_decay = jnp.exp(lg * cch)
        pair = slice(128 * (h // 2), 128 * (h // 2 + 1))
        vcol = slice(128 * h, 128 * (h + 1))
        lo = ROPE_DIM * (h % 2)
        head_lanes = (lane >= lo) & (lane < lo + ROPE_DIM)
        order = range(chunks - 1, -1, -1) if reverse else range(chunks)
        for c in order:
            rows = slice(cch * c, cch * (c + 1))
            qc = q_ref[0, rows, pair]
            kc = k_ref[0, rows, pair]
            kc = jnp.where(head_lanes, kc, jnp.zeros_like(kc))
            vc = v_ref[0, rows, vcol]
            sc = lax.dot_general(qc, kc, (((1,), (1,)), ((), ())), preferred_element_type=F32) * dmat
            o = jnp.dot(sc.astype(BF16), vc, preferred_element_type=F32)
            state = state_ref[h]
            o = o + xi * jnp.dot(qc, state.astype(BF16), preferred_element_type=F32)
            vz = (vc.astype(F32) * zeta).astype(BF16)
            kv = lax.dot_general(kc, vz, (((0,), (0,)), ((), ())), preferred_element_type=F32)
            state_ref[h] = chunk_decay * state + kv
            if reverse:
                t = o + of_ref[0, rows, vcol]
                mu = jnp.mean(t, axis=-1, keepdims=True)
                tc = t - mu
                var = jnp.mean(tc * tc, axis=-1, keepdims=True)
                y = tc * lax.rsqrt(var + NORM_EPS) * gng_ref[:, vcol] + gnb_ref[:, vcol]
                g = g_ref[0, rows, vcol].astype(F32)
                o_ref[0, rows, vcol] = (y * (g * _sigmoid(g))).astype(BF16)
            else:
                o_ref[0, rows, vcol] = o

    @pl.when(step == pl.num_programs(1) - 1)
    def _():
        sfin_ref[0] = state_ref[...]


def _retention_pass(decay, q, k, v, s0, post=None):
    b, n, _ = q.shape
    t = next(c for c in (512, 256, RET_CHUNK) if n % c == 0)
    steps = n // t
    reverse = post is not None
    blk = (lambda bi, i: (bi, steps - 1 - i, 0)) if reverse else (lambda bi, i: (bi, i, 0))
    const2 = lambda bi, i: (0, 0)
    state_spec = pl.BlockSpec((1, HEADS, 128, 128), lambda bi, i: (bi, 0, 0, 0))
    in_specs = [pl.BlockSpec((2, HEADS), const2),
                pl.BlockSpec((1, t, 256), blk), pl.BlockSpec((1, t, 256), blk), pl.BlockSpec((1, t, MIX_W), blk),
                state_spec]
    args = [decay, q, k, v, s0]
    if reverse:
        o_fwd, gate, gn_g, gn_b = post
        in_specs += [pl.BlockSpec((1, t, MIX_W), blk), pl.BlockSpec((1, t, MIX_W), blk),
                     pl.BlockSpec((1, MIX_W), const2), pl.BlockSpec((1, MIX_W), const2)]
        args += [o_fwd, gate, gn_g, gn_b]
    return pl.pallas_call(
        functools.partial(_ret_kernel, reverse=reverse, chunks=t // RET_CHUNK),
        grid=(b, steps),
        in_specs=in_specs,
        out_specs=[pl.BlockSpec((1, t, MIX_W), blk), state_spec],
        out_shape=[jax.ShapeDtypeStruct((b, n, MIX_W), BF16 if reverse else F32),
                   jax.ShapeDtypeStruct((b, HEADS, 128, 128), F32)],
        scratch_shapes=[pltpu.VMEM((HEADS, 128, 128), F32)],
        name="retention_bwd" if reverse else "retention_fwd",
        compiler_params=_params(("parallel", "arbitrary")),
    )(*args)


def _conv_kernel(prev_ref, cur_ref, next_ref, dw_ref, dwb_ref, lng_ref, lnb_ref, o_ref, buf_ref, *, t):
    i = pl.program_id(1)
    halo = CONV_HALO
    first = i == 0
    last = i == pl.num_programs(1) - 1
    prev = prev_ref[0].astype(F32)
    nxt = next_ref[0].astype(F32)
    buf_ref[0, 0:halo, :] = jnp.where(first, jnp.zeros_like(prev), prev)
    buf_ref[0, halo:halo + t, :] = cur_ref[0].astype(F32)
    buf_ref[0, halo + t:2 * halo + t, :] = jnp.where(last, jnp.zeros_like(nxt), nxt)
    span = t + 2 * halo - F32_SUBLANES
    for r in range(1, F32_SUBLANES):
        buf_ref[r, 0:span, :] = buf_ref[0, r:r + span, :]
    rb = 32
    off = halo - CONV_WIDTH // 2
    for blk in range(t // rb):
        acc = jnp.broadcast_to(dwb_ref[...], (rb, CONV_CH))
        for w in range(CONV_WIDTH):
            r, base = (off + w) % F32_SUBLANES, rb * blk + (off + w) // F32_SUBLANES * F32_SUBLANES
            acc = acc + buf_ref[r, base:base + rb, :] * dw_ref[w:w + 1, :]
        mu = jnp.mean(acc, axis=-1, keepdims=True)
        ac = acc - mu
        var = jnp.mean(ac * ac, axis=-1, keepdims=True)
        y = ac * lax.rsqrt(var + NORM_EPS) * lng_ref[...] + lnb_ref[...]
        o_ref[0, rb * blk:rb * (blk + 1), :] = (y * _sigmoid(y)).astype(BF16)


def _conv(u, dw, dw_b, ln_g, ln_b):
    b, n, ch = u.shape
    t = min(256, n)
    hb = t // CONV_HALO
    nh = n // CONV_HALO
    const2 = lambda bi, i: (0, 0)
    return pl.pallas_call(
        functools.partial(_conv_kernel, t=t),
        grid=(b, n // t),
        in_specs=[pl.BlockSpec((1, CONV_HALO, ch), lambda bi, i: (bi, jnp.maximum(i * hb - 1, 0), 0)),
                  pl.BlockSpec((1, t, ch), lambda bi, i: (bi, i, 0)),
                  pl.BlockSpec((1, CONV_HALO, ch), lambda bi, i: (bi, jnp.minimum((i + 1) * hb, nh - 1), 0)),
                  pl.BlockSpec((CONV_WIDTH, ch), const2),
                  pl.BlockSpec((1, ch), const2), pl.BlockSpec((1, ch), const2), pl.BlockSpec((1, ch), const2)],
        out_specs=pl.BlockSpec((1, t, ch), lambda bi, i: (bi, i, 0)),
        out_shape=jax.ShapeDtypeStruct((b, n, ch), BF16),
        scratch_shapes=[pltpu.VMEM((F32_SUBLANES, t + 2 * CONV_HALO, ch), F32)],
        name="conv",
        compiler_params=_params(("parallel", "parallel")),
    )(u, u, u, dw, dw_b, ln_g, ln_b)


def _merge_kernel(x_ref, mod_ref, gpost_ref, oa_ref, yr_ref, yc_ref, gt_ref,
                  wa_ref, wr_ref, wc_ref, wo_ref, o_ref):
    d = D_MODEL
    m = None
    for j, (y_ref, w_ref) in enumerate(((oa_ref, wa_ref), (yr_ref, wr_ref), (yc_ref, wc_ref))):
        y = jnp.dot(y_ref[0], w_ref[...], preferred_element_type=F32)
        term = _sigmoid(gt_ref[0, :, d * j:d * (j + 1)].astype(F32)) * y
        m = term if m is None else m + term
    y = jnp.dot(m.astype(BF16), wo_ref[...], preferred_element_type=F32)
    gate = mod_ref[0, 5:6, :]
    o_ref[0] = x_ref[0] + gate * _rms(y, gpost_ref[...])


def _merge(x, mods, mod_row, g_post, oa, yr, yc, gt, w_a, w_r, w_c, w_o, layer):
    b, n, d = x.shape
    tm = min(256, n)
    row = (lambda bi: bi) if mod_row is None else (lambda bi: mod_row)
    tok = lambda w: pl.BlockSpec((1, tm, w), lambda bi, i: (bi, i, 0))
    wsp = lambda k: _resident((None, k, d), lambda bi, i: (layer, 0, 0))
    return pl.pallas_call(
        _merge_kernel,
        grid=(b, n // tm),
        in_specs=[tok(d), pl.BlockSpec((1, N_ADA, d), lambda bi, i: (row(bi), 0, 0)),
                  pl.BlockSpec((1, d), lambda bi, i: (0, 0)),
                  tok(MIX_W), tok(MIX_W), tok(MIX_W), tok(N_BRANCH * d),
                  wsp(MIX_W), wsp(MIX_W), wsp(MIX_W), wsp(d)],
        out_specs=tok(d),
        out_shape=jax.ShapeDtypeStruct(x.shape, F32),
        name="merge",
        compiler_params=_params(("parallel", "parallel")),
    )(x, mods, g_post, oa, yr, yc, gt, w_a, w_r, w_c, w_o)


def kernel(x, c, ctx, c_ctx, ada_w, ada_b, norm_pre, norm_post, ffn1_up, ffn1_down, ffn2_up, ffn2_down, w_in, da_lambda, da_subln, da_proj, ret_decay, ret_gn_g, ret_gn_b, ret_proj, conv_dw, conv_dw_b, conv_ln_g, conv_ln_b, conv_proj, w_out):
    depth = ada_w.shape[0]
    b, n, d = x.shape
    ctx_row = b
    cvec = jnp.zeros((8, d), F32).at[:b].set(c).at[ctx_row].set(c_ctx)
    mods = _ada(cvec, ada_w, ada_b).reshape(depth, 8, N_ADA, d)
    tables = _rope_tables(n)
    wb = lambda w: w.astype(BF16)
    ffn1_up, ffn1_down, ffn2_up, ffn2_down = wb(ffn1_up), wb(ffn1_down), wb(ffn2_up), wb(ffn2_down)
    w_in, da_proj, ret_proj, conv_proj, w_out = wb(w_in), wb(da_proj), wb(ret_proj), wb(conv_proj), wb(w_out)
    zero_state = jnp.zeros((b, HEADS, 128, 128), F32)

    for l in range(depth):
        lam_init = 0.8 - 0.6 * math.exp(-0.3 * l)
        mod_l = mods[l]
        row = lambda a, j: a[l, j][None, :]
        ctx = _ffn(ctx, mod_l, ctx_row, row(norm_pre, 0), row(norm_post, 0), ffn1_up, ffn1_down, l, 0)
        x = _ffn(x, mod_l, None, row(norm_pre, 0), row(norm_post, 0), ffn1_up, ffn1_down, l, 0)
        n_ctx = ctx.shape[1]
        assert n_ctx == ATTN_BLOCK
        px = _inproj(x, mod_l, None, row(norm_pre, 1), w_in, l, tables=tables, lead_blocks=n_ctx // ATTN_BLOCK)
        pc = _inproj(ctx, mod_l, ctx_row, row(norm_pre, 1), w_in, l, prepend_to=(px[1], px[2]))
        k_all, vt_all = pc[1], pc[2]
        subln = da_subln[l][None, :]
        dec = ret_decay[l]
        gn_g, gn_b = ret_gn_g[l][None, :], ret_gn_b[l][None, :]
        conv_args = (conv_dw[l], conv_dw_b[l][None, :], conv_ln_g[l][None, :], conv_ln_b[l][None, :])
        oa_c = _attention(pc[0], k_all, vt_all, n_ctx, da_lambda[l], subln, lam_init)
        of_c, sf_c = _retention_pass(dec, pc[3], pc[4], pc[5], zero_state)
        yr_c, sb_c = _retention_pass(dec, pc[3], pc[4], pc[5], zero_state, post=(of_c, pc[6], gn_g, gn_b))
        yc_c = _conv(pc[7], *conv_args)
        oa_x = _attention(px[0], k_all, vt_all, n_ctx + n, da_lambda[l], subln, lam_init)
        of_x, _ = _retention_pass(dec, px[3], px[4], px[5], sf_c)
        yr_x, _ = _retention_pass(dec, px[3], px[4], px[5], sb_c, post=(of_x, px[6], gn_g, gn_b))
        yc_x = _conv(px[7], *conv_args)
        outs = []
        for tok, mrow, oa, yr, yc, gt in ((ctx, ctx_row, oa_c, yr_c, yc_c, pc[8]), (x, None, oa_x, yr_x, yc_x, px[8])):
            tok = _merge(tok, mod_l, mrow, row(norm_post, 1), oa, yr, yc, gt, da_proj, ret_proj, conv_proj, w_out, l)
            tok = _ffn(tok, mod_l, mrow, row(norm_pre, 2), row(norm_post, 2), ffn2_up, ffn2_down, l, 6)
            outs.append(tok)
        ctx, x = outs
    return x
```

```python
import functools
import math

import jax
import jax.numpy as jnp
from jax import lax
from jax.experimental import pallas as pl
from jax.experimental.pallas import tpu as pltpu

F32 = jnp.float32
BF16 = jnp.bfloat16

D_MODEL = 1024
N_ADA = 9
GRID_W = 64
ROPE_DIM = 64
ROPE_BASE = 10000.0
NORM_EPS = 1e-6
HEADS = 4
HEAD_LANES = 128
V_ROWS = HEAD_LANES + 16
ATTN_BLOCK = 256
ATTN_GROUP = 512
ATTN_Q_TILES = 2
ATTN_LEAD = 1
ATTN_SLOTS = ATTN_LEAD + 2
ATTN_STEPS_PER_TRIP = 15
ATTN_Q_SCALE = ROPE_DIM ** -0.5 * math.log2(math.e)
RET_CHUNK = 128
CONV_CH = 512
CONV_WIDTH = 31
CONV_HALO = 16
F32_SUBLANES = 8
D_FF = 2816
FFN_TILE = 512
MIX_W = 512
N_BRANCH = 3
_OFF = {}
_o = 0
for _name, _w in (("da_q", 512), ("da_k", 512), ("da_v", 512), ("r_q", 256), ("r_k", 256), ("r_v", 512),
                  ("r_g", 512), ("cv_a", 512), ("cv_g", 512), ("gt", N_BRANCH * D_MODEL)):
    _OFF[_name] = (_o, _o + _w)
    _o += _w
D_IN = _o

V7X_VMEM_LIMIT = 56 * 1024 * 1024


def _params(semantics, vmem=V7X_VMEM_LIMIT):
    return pltpu.CompilerParams(dimension_semantics=semantics, vmem_limit_bytes=vmem)


def _resident(shape, index_map):
    return pl.BlockSpec(shape, index_map, pipeline_mode=pl.Buffered(1))


def _sigmoid(x):
    return 1.0 / (1.0 + jnp.exp(-x))


def _rms(x, g):
    return x * lax.rsqrt(jnp.mean(x * x, axis=-1, keepdims=True) + NORM_EPS) * g


def _modulated(x, g_pre, mod_ref, k0):
    shift = mod_ref[0, k0:k0 + 1, :]
    scale = mod_ref[0, k0 + 1:k0 + 2, :]
    return _rms(x, g_pre) * (1.0 + scale) + shift


def _ada_kernel(c_ref, w_ref, b_ref, o_ref):
    c = c_ref[...]
    sc = c * _sigmoid(c)
    o_ref[0] = jnp.dot(sc, w_ref[0], preferred_element_type=F32) + b_ref[0]


def _ada(cvec, ada_w, ada_b):
    depth, d, n = ada_w.shape
    tn = 1024
    return pl.pallas_call(
        _ada_kernel,
        grid=(depth, n // tn),
        in_specs=[pl.BlockSpec((8, d), lambda l, j: (0, 0)),
                  pl.BlockSpec((1, d, tn), lambda l, j: (l, 0, j)),
                  pl.BlockSpec((1, 1, tn), lambda l, j: (l, 0, j))],
        out_specs=pl.BlockSpec((1, 8, tn), lambda l, j: (l, 0, j)),
        out_shape=jax.ShapeDtypeStruct((depth, 8, n), F32),
        name="ada",
        compiler_params=_params(("parallel", "parallel")),
    )(cvec, ada_w, ada_b.reshape(depth, 1, n))


def _ffn_kernel(x_ref, mod_ref, gpre_ref, gpost_ref, wup_ref, wdn_ref, o_ref, *, k0):
    x = x_ref[0]
    h = _modulated(x, gpre_ref[...], mod_ref, k0).astype(BF16)
    ab = jnp.dot(h, wup_ref[...], preferred_element_type=F32)
    a = ab[:, :D_FF]
    b = ab[:, D_FF:]
    u = (a * _sigmoid(a) * b).astype(BF16)
    y = jnp.dot(u, wdn_ref[...], preferred_element_type=F32)
    gate = mod_ref[0, k0 + 2:k0 + 3, :]
    o_ref[0] = x + 0.5 * gate * _rms(y, gpost_ref[...])


def _ffn(x, mods, mod_row, g_pre, g_post, w_up, w_dn, layer, k0):
    b, n, d = x.shape
    tm = FFN_TILE if n % FFN_TILE == 0 else 256
    row = (lambda bi: bi) if mod_row is None else (lambda bi: mod_row)
    return pl.pallas_call(
        functools.partial(_ffn_kernel, k0=k0),
        grid=(b, n // tm),
        in_specs=[pl.BlockSpec((1, tm, d), lambda bi, i: (bi, i, 0)),
                  pl.BlockSpec((1, N_ADA, d), lambda bi, i: (row(bi), 0, 0)),
                  pl.BlockSpec((1, d), lambda bi, i: (0, 0)),
                  pl.BlockSpec((1, d), lambda bi, i: (0, 0)),
                  _resident((None, d, 2 * D_FF), lambda bi, i: (layer, 0, 0)),
                  _resident((None, D_FF, d), lambda bi, i: (layer, 0, 0))],
        out_specs=pl.BlockSpec((1, tm, d), lambda bi, i: (bi, i, 0)),
        out_shape=jax.ShapeDtypeStruct(x.shape, F32),
        name="ffn",
        compiler_params=_params(("parallel", "parallel")),
    )(x, mods, g_pre, g_post, w_up, w_dn)


def _swap16(c):
    lane = lax.broadcasted_iota(jnp.int32, c.shape, 1)
    return jnp.where((lane & 16) != 0, pltpu.roll(c, 16, 1), pltpu.roll(c, 112, 1))


def _inproj_kernel(*refs, rope):
    if rope:
        x_ref, mod_ref, gpre_ref, w_ref, cos_ref, sin_ref = refs[:6]
        outs = refs[6:]
    else:
        x_ref, mod_ref, gpre_ref, w_ref = refs[:4]
        outs = refs[6:]
    daq_ref, dak_ref, dav_ref, rq_ref, rk_ref, rv_ref, rg_ref, cu_ref, gt_ref = outs
    h = _modulated(x_ref[0], gpre_ref[...], mod_ref, 3).astype(BF16)
    y = jnp.dot(h, w_ref[...], preferred_element_type=F32)

    def rotary(name, out_ref, mul, transposed=False):
        lo, hi = _OFF[name]
        for j in range((hi - lo) // 128):
            c = y[:, lo + 128 * j:lo + 128 * (j + 1)]
            if rope:
                c = c * cos_ref[...] + _swap16(c) * sin_ref[...]
            if mul != 1.0:
                c = c * mul
            if transposed:
                out_ref[0, 128 * j:128 * (j + 1), :] = c.T.astype(BF16)
            else:
                out_ref[0, :, 128 * j:128 * (j + 1)] = c.astype(BF16)

    def plain(name, out_ref):
        lo, hi = _OFF[name]
        out_ref[0] = y[:, lo:hi].astype(BF16)

    rotary("da_q", daq_ref, ATTN_Q_SCALE, transposed=True)
    rotary("da_k", dak_ref, 1.0)
    lo, hi = _OFF["da_v"]
    for j in range((hi - lo) // 128):
        base = V_ROWS * j
        dav_ref[0, base:base + 128, :] = y[:, lo + 128 * j:lo + 128 * (j + 1)].T.astype(BF16)
        pad = lax.broadcasted_iota(jnp.int32, (V_ROWS - 128, y.shape[0]), 0)
        dav_ref[0, base + 128:base + V_ROWS, :] = jnp.where(pad == 0, 1.0, 0.0).astype(BF16)
    rotary("r_q", rq_ref, ROPE_DIM ** -0.5)
    rotary("r_k", rk_ref, 1.0)
    plain("r_v", rv_ref)
    plain("r_g", rg_ref)
    a = y[:, _OFF["cv_a"][0]:_OFF["cv_a"][1]]
    g = y[:, _OFF["cv_g"][0]:_OFF["cv_g"][1]]
    cu_ref[0] = (a * _sigmoid(g)).astype(BF16)
    plain("gt", gt_ref)


_INPROJ_WIDTHS = (512, 512, 512, 256, 256, 512, 512, 512, N_BRANCH * D_MODEL)


def _inproj(x, mods, mod_row, g_pre, w_in, layer, tables=None, prepend_to=None, lead_blocks=0):
    b, n, d = x.shape
    tm = ATTN_BLOCK
    nb = n // tm
    rope = tables is not None
    if prepend_to is None:
        kv_blocks, kv_off = nb + lead_blocks, lead_blocks
    else:
        kv_blocks, kv_off = prepend_to[0].shape[1] // tm, 0
    token_major = lambda w: (pl.BlockSpec((1, tm, w), lambda bi, i: (bi, i, 0)), jax.ShapeDtypeStruct((b, n, w), BF16))
    outs = [(pl.BlockSpec((1, 512, tm), lambda bi, i: (bi, 0, i)), jax.ShapeDtypeStruct((b, 512, n), BF16)),
            (pl.BlockSpec((1, tm, 512), lambda bi, i: (bi, i + kv_off, 0)),
             jax.ShapeDtypeStruct((b, kv_blocks * tm, 512), BF16)),
            (pl.BlockSpec((1, HEADS * V_ROWS, tm), lambda bi, i: (bi, 0, i + kv_off)),
             jax.ShapeDtypeStruct((b, HEADS * V_ROWS, kv_blocks * tm), BF16))]
    outs += [token_major(w) for w in _INPROJ_WIDTHS[3:]]
    row = (lambda bi: bi) if mod_row is None else (lambda bi: mod_row)
    in_specs = [pl.BlockSpec((1, tm, d), lambda bi, i: (bi, i, 0)),
                pl.BlockSpec((1, N_ADA, d), lambda bi, i: (row(bi), 0, 0)),
                pl.BlockSpec((1, d), lambda bi, i: (0, 0)),
                _resident((None, d, D_IN), lambda bi, i: (layer, 0, 0))]
    args = [x, mods, g_pre, w_in]
    aliases = {}
    if rope:
        in_specs += [pl.BlockSpec((tm, 128), lambda bi, i: (i, 0))] * 2
        args += list(tables)
    else:
        in_specs += [pl.BlockSpec(memory_space=pl.ANY)] * 2
        args += list(prepend_to)
        aliases = {4: 1, 5: 2}
    return pl.pallas_call(
        functools.partial(_inproj_kernel, rope=rope),
        grid=(b, nb),
        in_specs=in_specs,
        out_specs=[o[0] for o in outs],
        out_shape=[o[1] for o in outs],
        input_output_aliases=aliases,
        name="inproj",
        compiler_params=_params(("parallel", "parallel")),
    )(*args)


def _rope_tables(n):
    t = jnp.arange(n, dtype=jnp.int32)
    row = (t // GRID_W).astype(F32)
    col = (t % GRID_W).astype(F32)
    axis_dim = ROPE_DIM // 2
    inv_freq = ROPE_BASE ** (-jnp.arange(0, axis_dim, 2, dtype=F32) / axis_dim)
    ang_r = row[:, None] * inv_freq
    ang_c = col[:, None] * inv_freq
    cos64 = jnp.concatenate([jnp.cos(ang_r), jnp.cos(ang_r), jnp.cos(ang_c), jnp.cos(ang_c)], axis=-1)
    sin64 = jnp.concatenate([-jnp.sin(ang_r), jnp.sin(ang_r), -jnp.sin(ang_c), jnp.sin(ang_c)], axis=-1)
    return jnp.tile(cos64, (1, 2)), jnp.tile(sin64, (1, 2))


def _attn_kernel(lam_ref, subln_ref, q_ref, k_ref, vt_ref, o_ref, *scratch,
                 head, group, n_groups, q_tiles, lam_init):
    s_refs = scratch[:ATTN_SLOTS]
    m_ref, mu_ref = scratch[ATTN_SLOTS:ATTN_SLOTS + 2]
    acc_refs = scratch[ATTN_SLOTS + 2:]
    tq = ATTN_BLOCK
    qmaps = []
    for t in range(q_tiles):
        qt = q_ref[0, :, tq * t:tq * (t + 1)]
        row = lax.broadcasted_iota(jnp.int32, qt.shape, 0)
        zero = jnp.zeros_like(qt)
        qmaps += [jnp.where(row < ROPE_DIM, qt, zero), jnp.where(row >= ROPE_DIM, qt, zero)]
    n_streams = len(qmaps)
    row0 = pl.multiple_of(jnp.minimum(pl.program_id(2), 0), ATTN_GROUP)

    def rows(size):
        return pl.ds(row0, size)

    n_units = n_groups + 1

    def span(v):
        if isinstance(v, int):
            return (0, head) if v == 0 else (head + (v - 1) * group, group)
        return pl.multiple_of(head + (v - 1) * group, ATTN_BLOCK), group

    def scores(v, slot):
        start, size = span(v)
        k = k_ref[0, pl.ds(start, size), :]
        for c in range(n_streams):
            s = jnp.dot(k, qmaps[c], preferred_element_type=F32)
            s_refs[slot][c, :size] = s
            mu_ref[slot, c] = jnp.max(s, axis=0, keepdims=True)

    def softmax_values(v, slot):
        first = isinstance(v, int) and v == 0
        start, size = span(v)
        vt = vt_ref[0, :, pl.ds(start, size)]
        for c in range(n_streams):
            m_unit = mu_ref[slot, c]
            if first:
                m_new = m_unit
            else:
                m_old = m_ref[c]
                m_new = jnp.maximum(m_old, m_unit)
                alpha = jnp.exp2(m_old - m_new)
            m_ref[c] = m_new
            p = jnp.exp2(s_refs[slot][c, rows(size)] - m_new).astype(BF16)
            pv = jnp.dot(vt, p, preferred_element_type=F32)
            acc_refs[c][...] = pv if first else alpha * acc_refs[c][rows(V_ROWS)] + pv

    def step(v, cur):
        if not isinstance(v, int) or v + ATTN_LEAD < n_units:
            scores(v + ATTN_LEAD, (cur + ATTN_LEAD) % ATTN_SLOTS)
        softmax_values(v, cur)

    for v in range(min(ATTN_LEAD, n_units)):
        scores(v, v % ATTN_SLOTS)
    first_loop, end_loop = 1, n_units - ATTN_LEAD
    trips = max(end_loop - first_loop, 0) // ATTN_STEPS_PER_TRIP
    for v in range(min(first_loop, n_units)):
        step(v, v % ATTN_SLOTS)
    if trips:
        def body(i, carry):
            for t in range(ATTN_STEPS_PER_TRIP):
                step(first_loop + ATTN_STEPS_PER_TRIP * i + t, (first_loop + t) % ATTN_SLOTS)
            return carry

        lax.fori_loop(0, trips, body, 0)
    for v in range(first_loop + trips * ATTN_STEPS_PER_TRIP, n_units):
        step(v, v % ATTN_SLOTS)

    lv = lam_ref[...]
    lam = (jnp.exp(jnp.sum(lv[0:1] * lv[1:2], axis=-1, keepdims=True))
           - jnp.exp(jnp.sum(lv[2:3] * lv[3:4], axis=-1, keepdims=True)) + lam_init)
    hl = HEAD_LANES
    normed = [acc_ref[:hl] / acc_ref[hl:hl + 1] for acc_ref in acc_refs]
    for t in range(q_tiles):
        ot = normed[2 * t] - lam * normed[2 * t + 1]
        o_ref[0, tq * t:tq * (t + 1), :] = (_rms(ot.T, subln_ref[...]) * (1.0 - lam_init)).astype(BF16)


def _attention(q, k, vt, n_keys, da_lambda, subln, lam_init):
    b, _, nq = q.shape
    hl = HEAD_LANES
    q_tiles = ATTN_Q_TILES if nq % (ATTN_Q_TILES * ATTN_BLOCK) == 0 else 1
    tq = q_tiles * ATTN_BLOCK
    n_streams = 2 * q_tiles
    head = ATTN_BLOCK
    rest = n_keys - head
    group = ATTN_GROUP if rest % ATTN_GROUP == 0 else ATTN_BLOCK
    rows = max(group, head)
    in_specs = [pl.BlockSpec((4, ROPE_DIM), lambda bi, h, i: (0, 0)),
                pl.BlockSpec((1, hl), lambda bi, h, i: (0, 0)),
                pl.BlockSpec((1, hl, tq), lambda bi, h, i: (bi, h, i)),
                pl.BlockSpec((1, n_keys, hl), lambda bi, h, i: (bi, 0, h)),
                pl.BlockSpec((1, V_ROWS, n_keys), lambda bi, h, i: (bi, h, 0))]
    args = [da_lambda, subln, q, k, vt]
    return pl.pallas_call(
        functools.partial(_attn_kernel, head=head, group=group, n_groups=rest // group, q_tiles=q_tiles,
                          lam_init=lam_init),
        grid=(b, HEADS, nq // tq),
        in_specs=in_specs,
        out_specs=pl.BlockSpec((1, tq, hl), lambda bi, h, i: (bi, i, h)),
        out_shape=jax.ShapeDtypeStruct((b, nq, MIX_W), BF16),
        scratch_shapes=(
            [pltpu.VMEM((n_streams, rows, ATTN_BLOCK), F32)] * ATTN_SLOTS
            + [pltpu.VMEM((n_streams, 1, ATTN_BLOCK), F32)]
            + [pltpu.VMEM((ATTN_SLOTS, n_streams, 1, ATTN_BLOCK), F32)]
            + [pltpu.VMEM((V_ROWS, ATTN_BLOCK), F32)] * n_streams),
        name="attn",
        compiler_params=_params(("parallel", "parallel", "arbitrary")),
    )(*args)


def _ret_kernel(*refs, reverse, chunks):
    if reverse:
        (dec_ref, q_ref, k_ref, v_ref, s0_ref, of_ref, g_ref, gng_ref, gnb_ref,
         o_ref, sfin_ref, state_ref) = refs
    else:
        dec_ref, q_ref, k_ref, v_ref, s0_ref, o_ref, sfin_ref, state_ref = refs
    step = pl.program_id(0)
    batch = q_ref.shape[0]

    @pl.when(step == 0)
    def _():
        state_ref[...] = s0_ref[...]

    cch = RET_CHUNK
    ii = lax.broadcasted_iota(jnp.int32, (cch, cch), 0).astype(F32)
    ss = lax.broadcasted_iota(jnp.int32, (cch, cch), 1).astype(F32)
    pos = lax.broadcasted_iota(jnp.int32, (cch, 1), 0).astype(F32)
    lane = lax.broadcasted_iota(jnp.int32, (cch, 2 * ROPE_DIM), 1)
    d_row = 1 if reverse else 0
    for h in range(HEADS):
        x = dec_ref[d_row:d_row + 1, h:h + 1]
        lg = jnp.minimum(x, 0.0) - jnp.log(1.0 + jnp.exp(-jnp.abs(x)))
        if reverse:
            dist = ss - ii
            dmat = jnp.where(dist > 0, jnp.exp(lg * jnp.maximum(dist, 0.0)), 0.0)
            zeta = jnp.exp(lg * pos)
            xi = jnp.exp(lg * (cch - pos))
        else:
            dist = ii - ss
            dmat = jnp.where(dist >= 0, jnp.exp(lg * jnp.maximum(dist, 0.0)), 0.0)
            zeta = jnp.exp(lg * (cch - 1 - pos))
            xi = jnp.exp(lg * (pos + 1.0))
        chunk_decay = jnp.exp(lg * cch)
        pair = slice(128 * (h // 2), 128 * (h // 2 + 1))
        vcol = slice(128 * h, 128 * (h + 1))
        lo = ROPE_DIM * (h % 2)
        head_lanes = (lane >= lo) & (lane < lo + ROPE_DIM)
        order = range(chunks - 1, -1, -1) if reverse else range(chunks)
        for c in order:
            rows = slice(cch * c, cch * (c + 1))
            for bi in range(batch):
                qc = q_ref[bi, rows, pair]
                kc = k_ref[bi, rows, pair]
                kc = jnp.where(head_lanes, kc, jnp.zeros_like(kc))
                vc = v_ref[bi, rows, vcol]
                sc = lax.dot_general(qc, kc, (((1,), (1,)), ((), ())), preferred_element_type=F32) * dmat
                o = jnp.dot(sc.astype(BF16), vc, preferred_element_type=F32)
                state = state_ref[bi, h]
                o = o + xi * jnp.dot(qc, state.astype(BF16), preferred_element_type=F32)
                vz = (vc.astype(F32) * zeta).astype(BF16)
                kv = lax.dot_general(kc, vz, (((0,), (0,)), ((), ())), preferred_element_type=F32)
                state_ref[bi, h] = chunk_decay * state + kv
                if reverse:
                    t = o + of_ref[bi, rows, vcol]
                    mu = jnp.mean(t, axis=-1, keepdims=True)
                    tc = t - mu
                    var = jnp.mean(tc * tc, axis=-1, keepdims=True)
                    y = tc * lax.rsqrt(var + NORM_EPS) * gng_ref[:, vcol] + gnb_ref[:, vcol]
                    g = g_ref[bi, rows, vcol].astype(F32)
                    o_ref[bi, rows, vcol] = (y * (g * _sigmoid(g))).astype(BF16)
                else:
                    o_ref[bi, rows, vcol] = o

    @pl.when(step == pl.num_programs(0) - 1)
    def _():
        sfin_ref[...] = state_ref[...]


def _retention_pass(decay, q, k, v, s0, post=None):
    b, n, _ = q.shape
    t = next(c for c in (512, 256, RET_CHUNK) if n % c == 0)
    steps = n // t
    reverse = post is not None
    blk = (lambda i: (0, steps - 1 - i, 0)) if reverse else (lambda i: (0, i, 0))
    const2 = lambda i: (0, 0)
    state_spec = pl.BlockSpec((b, HEADS, 128, 128), lambda i: (0, 0, 0, 0))
    in_specs = [pl.BlockSpec((2, HEADS), const2),
                pl.BlockSpec((b, t, 256), blk), pl.BlockSpec((b, t, 256), blk), pl.BlockSpec((b, t, MIX_W), blk),
                state_spec]
    args = [decay, q, k, v, s0]
    if reverse:
        o_fwd, gate, gn_g, gn_b = post
        in_specs += [pl.BlockSpec((b, t, MIX_W), blk), pl.BlockSpec((b, t, MIX_W), blk),
                     pl.BlockSpec((1, MIX_W), const2), pl.BlockSpec((1, MIX_W), const2)]
        args += [o_fwd, gate, gn_g, gn_b]
    return pl.pallas_call(
        functools.partial(_ret_kernel, reverse=reverse, chunks=t // RET_CHUNK),
        grid=(steps,),
        in_specs=in_specs,
        out_specs=[pl.BlockSpec((b, t, MIX_W), blk), state_spec],
        out_shape=[jax.ShapeDtypeStruct((b, n, MIX_W), BF16 if reverse else F32),
                   jax.ShapeDtypeStruct((b, HEADS, 128, 128), F32)],
        scratch_shapes=[pltpu.VMEM((b, HEADS, 128, 128), F32)],
        name="retention_bwd" if reverse else "retention_fwd",
        compiler_params=_params(("arbitrary",)),
    )(*args)


def _conv_kernel(prev_ref, cur_ref, next_ref, dw_ref, dwb_ref, lng_ref, lnb_ref, o_ref, buf_ref, *, t):
    i = pl.program_id(1)
    halo = CONV_HALO
    first = i == 0
    last = i == pl.num_programs(1) - 1
    prev = prev_ref[0].astype(F32)
    nxt = next_ref[0].astype(F32)
    buf_ref[0, 0:halo, :] = jnp.where(first, jnp.zeros_like(prev), prev)
    buf_ref[0, halo:halo + t, :] = cur_ref[0].astype(F32)
    buf_ref[0, halo + t:2 * halo + t, :] = jnp.where(last, jnp.zeros_like(nxt), nxt)
    span = t + 2 * halo - F32_SUBLANES
    for r in range(1, F32_SUBLANES):
        buf_ref[r, 0:span, :] = buf_ref[0, r:r + span, :]
    rb = 32
    off = halo - CONV_WIDTH // 2
    for blk in range(t // rb):
        acc = jnp.broadcast_to(dwb_ref[...], (rb, CONV_CH))
        for w in range(CONV_WIDTH):
            r, base = (off + w) % F32_SUBLANES, rb * blk + (off + w) // F32_SUBLANES * F32_SUBLANES
            acc = acc + buf_ref[r, base:base + rb, :] * dw_ref[w:w + 1, :]
        mu = jnp.mean(acc, axis=-1, keepdims=True)
        ac = acc - mu
        var = jnp.mean(ac * ac, axis=-1, keepdims=True)
        y = ac * lax.rsqrt(var + NORM_EPS) * lng_ref[...] + lnb_ref[...]
        o_ref[0, rb * blk:rb * (blk + 1), :] = (y * _sigmoid(y)).astype(BF16)


def _conv(u, dw, dw_b, ln_g, ln_b):
    b, n, ch = u.shape
    t = min(256, n)
    hb = t // CONV_HALO
    nh = n // CONV_HALO
    const2 = lambda bi, i: (0, 0)
    return pl.pallas_call(
        functools.partial(_conv_kernel, t=t),
        grid=(b, n // t),
        in_specs=[pl.BlockSpec((1, CONV_HALO, ch), lambda bi, i: (bi, jnp.maximum(i * hb - 1, 0), 0)),
                  pl.BlockSpec((1, t, ch), lambda bi, i: (bi, i, 0)),
                  pl.BlockSpec((1, CONV_HALO, ch), lambda bi, i: (bi, jnp.minimum((i + 1) * hb, nh - 1), 0)),
                  pl.BlockSpec((CONV_WIDTH, ch), const2),
                  pl.BlockSpec((1, ch), const2), pl.BlockSpec((1, ch), const2), pl.BlockSpec((1, ch), const2)],
        out_specs=pl.BlockSpec((1, t, ch), lambda bi, i: (bi, i, 0)),
        out_shape=jax.ShapeDtypeStruct((b, n, ch), BF16),
        scratch_shapes=[pltpu.VMEM((F32_SUBLANES, t + 2 * CONV_HALO, ch), F32)],
        name="conv",
        compiler_params=_params(("parallel", "parallel")),
    )(u, u, u, dw, dw_b, ln_g, ln_b)


def _merge_kernel(x_ref, mod_ref, gpost_ref, oa_ref, yr_ref, yc_ref, gt_ref,
                  wa_ref, wr_ref, wc_ref, wo_ref, o_ref):
    d = D_MODEL
    m = None
    for j, (y_ref, w_ref) in enumerate(((oa_ref, wa_ref), (yr_ref, wr_ref), (yc_ref, wc_ref))):
        y = jnp.dot(y_ref[0], w_ref[...], preferred_element_type=F32)
        term = _sigmoid(gt_ref[0, :, d * j:d * (j + 1)].astype(F32)) * y
        m = term if m is None else m + term
    y = jnp.dot(m.astype(BF16), wo_ref[...], preferred_element_type=F32)
    gate = mod_ref[0, 5:6, :]
    o_ref[0] = x_ref[0] + gate * _rms(y, gpost_ref[...])


def _merge(x, mods, mod_row, g_post, oa, yr, yc, gt, w_a, w_r, w_c, w_o, layer):
    b, n, d = x.shape
    tm = min(256, n)
    row = (lambda bi: bi) if mod_row is None else (lambda bi: mod_row)
    tok = lambda w: pl.BlockSpec((1, tm, w), lambda bi, i: (bi, i, 0))
    wsp = lambda k: _resident((None, k, d), lambda bi, i: (layer, 0, 0))
    return pl.pallas_call(
        _merge_kernel,
        grid=(b, n // tm),
        in_specs=[tok(d), pl.BlockSpec((1, N_ADA, d), lambda bi, i: (row(bi), 0, 0)),
                  pl.BlockSpec((1, d), lambda bi, i: (0, 0)),
                  tok(MIX_W), tok(MIX_W), tok(MIX_W), tok(N_BRANCH * d),
                  wsp(MIX_W), wsp(MIX_W), wsp(MIX_W), wsp(d)],
        out_specs=tok(d),
        out_shape=jax.ShapeDtypeStruct(x.shape, F32),
        name="merge",
        compiler_params=_params(("parallel", "parallel")),
    )(x, mods, g_post, oa, yr, yc, gt, w_a, w_r, w_c, w_o)


def kernel(x, c, ctx, c_ctx, ada_w, ada_b, norm_pre, norm_post, ffn1_up, ffn1_down, ffn2_up, ffn2_down, w_in, da_lambda, da_subln, da_proj, ret_decay, ret_gn_g, ret_gn_b, ret_proj, conv_dw, conv_dw_b, conv_ln_g, conv_ln_b, conv_proj, w_out):
    depth = ada_w.shape[0]
    b, n, d = x.shape
    ctx_row = b
    cvec = jnp.zeros((8, d), F32).at[:b].set(c).at[ctx_row].set(c_ctx)
    mods = _ada(cvec, ada_w, ada_b).reshape(depth, 8, N_ADA, d)
    tables = _rope_tables(n)
    wb = lambda w: w.astype(BF16)
    ffn1_up, ffn1_down, ffn2_up, ffn2_down = wb(ffn1_up), wb(ffn1_down), wb(ffn2_up), wb(ffn2_down)
    w_in, da_proj, ret_proj, conv_proj, w_out = wb(w_in), wb(da_proj), wb(ret_proj), wb(conv_proj), wb(w_out)
    zero_state = jnp.zeros((b, HEADS, 128, 128), F32)

    for l in range(depth):
        lam_init = 0.8 - 0.6 * math.exp(-0.3 * l)
        mod_l = mods[l]
        row = lambda a, j: a[l, j][None, :]
        ctx = _ffn(ctx, mod_l, ctx_row, row(norm_pre, 0), row(norm_post, 0), ffn1_up, ffn1_down, l, 0)
        x = _ffn(x, mod_l, None, row(norm_pre, 0), row(norm_post, 0), ffn1_up, ffn1_down, l, 0)
        n_ctx = ctx.shape[1]
        assert n_ctx == ATTN_BLOCK
        px = _inproj(x, mod_l, None, row(norm_pre, 1), w_in, l, tables=tables, lead_blocks=n_ctx // ATTN_BLOCK)
        pc = _inproj(ctx, mod_l, ctx_row, row(norm_pre, 1), w_in, l, prepend_to=(px[1], px[2]))
        k_all, vt_all = pc[1], pc[2]
        subln = da_subln[l][None, :]
        dec = ret_decay[l]
        gn_g, gn_b = ret_gn_g[l][None, :], ret_gn_b[l][None, :]
        conv_args = (conv_dw[l], conv_dw_b[l][None, :], conv_ln_g[l][None, :], conv_ln_b[l][None, :])
        oa_c = _attention(pc[0], k_all, vt_all, n_ctx, da_lambda[l], subln, lam_init)
        of_c, sf_c = _retention_pass(dec, pc[3], pc[4], pc[5], zero_state)
        yr_c, sb_c = _retention_pass(dec, pc[3], pc[4], pc[5], zero_state, post=(of_c, pc[6], gn_g, gn_b))
        yc_c = _conv(pc[7], *conv_args)
        oa_x = _attention(px[0], k_all, vt_all, n_ctx + n, da_lambda[l], subln, lam_init)
        of_x, _ = _retention_pass(dec, px[3], px[4], px[5], sf_c)
        yr_x, _ = _retention_pass(dec, px[3], px[4], px[5], sb_c, post=(of_x, px[6], gn_g, gn_b))
        yc_x = _conv(px[7], *conv_args)
        outs = []
        for tok, mrow, oa, yr, yc, gt in ((ctx, ctx_row, oa_c, yr_c, yc_c, pc[8]), (x, None, oa_x, yr_x, yc_x, px[8])):
            tok = _merge(tok, mod_l, mrow, row(norm_post, 1), oa, yr, yc, gt, da_proj, ret_proj, conv_proj, w_out, l)
            tok = _ffn(tok, mod_l, mrow, row(norm_pre, 2), row(norm_post, 2), ffn2_up, ffn2_down, l, 6)
            outs.append(tok)
        ctx, x = outs
    return x
```

```python
import functools
import math

import jax
import jax.numpy as jnp
from jax import lax
from jax.experimental import pallas as pl
from jax.experimental.pallas import tpu as pltpu

F32 = jnp.float32
BF16 = jnp.bfloat16

D_MODEL = 1024
N_ADA = 9
GRID_W = 64
ROPE_DIM = 64
ROPE_BASE = 10000.0
NORM_EPS = 1e-6
HEADS = 4
HEAD_LANES = 128
V_ROWS = HEAD_LANES + 16
ATTN_BLOCK = 256
ATTN_GROUP = 512
ATTN_Q_TILES = 2
ATTN_LEAD = 1
ATTN_SLOTS = ATTN_LEAD + 2
ATTN_STEPS_PER_TRIP = 30
ATTN_Q_SCALE = ROPE_DIM ** -0.5 * math.log2(math.e)
RET_CHUNK = 128
CONV_CH = 512
CONV_WIDTH = 31
CONV_HALO = 16
F32_SUBLANES = 8
D_FF = 2816
FFN_TILE = 512
MERGE_TILE = 1024
MERGE_ROWS = 512
MIX_W = 512
N_BRANCH = 3
_OFF = {}
_o = 0
for _name, _w in (("da_q", 512), ("da_k", 512), ("da_v", 512), ("r_q", 256), ("r_k", 256), ("r_v", 512),
                  ("r_g", 512), ("cv_a", 512), ("cv_g", 512), ("gt", N_BRANCH * D_MODEL)):
    _OFF[_name] = (_o, _o + _w)
    _o += _w
D_IN = _o

V7X_VMEM_LIMIT = 56 * 1024 * 1024


def _params(semantics, vmem=V7X_VMEM_LIMIT):
    return pltpu.CompilerParams(dimension_semantics=semantics, vmem_limit_bytes=vmem)


def _resident(shape, index_map):
    return pl.BlockSpec(shape, index_map, pipeline_mode=pl.Buffered(1))


def _sigmoid(x):
    return 1.0 / (1.0 + jnp.exp(-x))


def _rms(x, g):
    return x * lax.rsqrt(jnp.mean(x * x, axis=-1, keepdims=True) + NORM_EPS) * g


def _modulated(x, g_pre, mod_ref, k0):
    shift = mod_ref[0, k0:k0 + 1, :]
    scale = mod_ref[0, k0 + 1:k0 + 2, :]
    return _rms(x, g_pre) * (1.0 + scale) + shift


def _ada_kernel(c_ref, w_ref, b_ref, o_ref):
    c = c_ref[...]
    sc = c * _sigmoid(c)
    o_ref[0] = jnp.dot(sc, w_ref[0], preferred_element_type=F32) + b_ref[0]


def _ada(cvec, ada_w, ada_b):
    depth, d, n = ada_w.shape
    tn = 1024
    return pl.pallas_call(
        _ada_kernel,
        grid=(depth, n // tn),
        in_specs=[pl.BlockSpec((8, d), lambda l, j: (0, 0)),
                  pl.BlockSpec((1, d, tn), lambda l, j: (l, 0, j)),
                  pl.BlockSpec((1, 1, tn), lambda l, j: (l, 0, j))],
        out_specs=pl.BlockSpec((1, 8, tn), lambda l, j: (l, 0, j)),
        out_shape=jax.ShapeDtypeStruct((depth, 8, n), F32),
        name="ada",
        compiler_params=_params(("parallel", "parallel")),
    )(cvec, ada_w, ada_b.reshape(depth, 1, n))


def _ffn_kernel(x_ref, mod_ref, gpre_ref, gpost_ref, wup_ref, wdn_ref, o_ref, *, k0):
    x = x_ref[0]
    h = _modulated(x, gpre_ref[...], mod_ref, k0).astype(BF16)
    ab = jnp.dot(h, wup_ref[...], preferred_element_type=F32)
    a = ab[:, :D_FF]
    b = ab[:, D_FF:]
    u = (a * _sigmoid(a) * b).astype(BF16)
    y = jnp.dot(u, wdn_ref[...], preferred_element_type=F32)
    gate = mod_ref[0, k0 + 2:k0 + 3, :]
    o_ref[0] = x + 0.5 * gate * _rms(y, gpost_ref[...])


def _ffn(x, mods, mod_row, g_pre, g_post, w_up, w_dn, layer, k0):
    b, n, d = x.shape
    tm = FFN_TILE if n % FFN_TILE == 0 else 256
    row = (lambda bi: bi) if mod_row is None else (lambda bi: mod_row)
    return pl.pallas_call(
        functools.partial(_ffn_kernel, k0=k0),
        grid=(b, n // tm),
        in_specs=[pl.BlockSpec((1, tm, d), lambda bi, i: (bi, i, 0)),
                  pl.BlockSpec((1, N_ADA, d), lambda bi, i: (row(bi), 0, 0)),
                  pl.BlockSpec((1, d), lambda bi, i: (0, 0)),
                  pl.BlockSpec((1, d), lambda bi, i: (0, 0)),
                  _resident((None, d, 2 * D_FF), lambda bi, i: (layer, 0, 0)),
                  _resident((None, D_FF, d), lambda bi, i: (layer, 0, 0))],
        out_specs=pl.BlockSpec((1, tm, d), lambda bi, i: (bi, i, 0)),
        out_shape=jax.ShapeDtypeStruct(x.shape, F32),
        name="ffn",
        compiler_params=_params(("parallel", "parallel")),
    )(x, mods, g_pre, g_post, w_up, w_dn)


def _swap16(c):
    lane = lax.broadcasted_iota(jnp.int32, c.shape, 1)
    return jnp.where((lane & 16) != 0, pltpu.roll(c, 16, 1), pltpu.roll(c, 112, 1))


def _inproj_kernel(*refs, rope):
    if rope:
        x_ref, mod_ref, gpre_ref, w_ref, cos_ref, sin_ref = refs[:6]
        outs = refs[6:]
    else:
        x_ref, mod_ref, gpre_ref, w_ref = refs[:4]
        outs = refs[4:]
    daq_ref, dak_ref, dav_ref, rq_ref, rk_ref, rv_ref, rg_ref, cu_ref, gt_ref = outs
    h = _modulated(x_ref[0], gpre_ref[...], mod_ref, 3).astype(BF16)
    y = jnp.dot(h, w_ref[...], preferred_element_type=F32)

    def rotary(name, out_ref, mul, transposed=False):
        lo, hi = _OFF[name]
        for j in range((hi - lo) // 128):
            c = y[:, lo + 128 * j:lo + 128 * (j + 1)]
            if rope:
                c = c * cos_ref[...] + _swap16(c) * sin_ref[...]
            if mul != 1.0:
                c = c * mul
            if transposed:
                out_ref[0, 128 * j:128 * (j + 1), :] = c.T.astype(BF16)
            else:
                out_ref[0, :, 128 * j:128 * (j + 1)] = c.astype(BF16)

    def plain(name, out_ref):
        lo, hi = _OFF[name]
        out_ref[0] = y[:, lo:hi].astype(BF16)

    rotary("da_q", daq_ref, ATTN_Q_SCALE, transposed=True)
    rotary("da_k", dak_ref, 1.0)
    lo, hi = _OFF["da_v"]
    for j in range((hi - lo) // 128):
        base = V_ROWS * j
        dav_ref[0, base:base + 128, :] = y[:, lo + 128 * j:lo + 128 * (j + 1)].T.astype(BF16)
        pad = lax.broadcasted_iota(jnp.int32, (V_ROWS - 128, y.shape[0]), 0)
        dav_ref[0, base + 128:base + V_ROWS, :] = jnp.where(pad == 0, 1.0, 0.0).astype(BF16)
    rotary("r_q", rq_ref, ROPE_DIM ** -0.5)
    rotary("r_k", rk_ref, 1.0)
    plain("r_v", rv_ref)
    plain("r_g", rg_ref)
    a = y[:, _OFF["cv_a"][0]:_OFF["cv_a"][1]]
    g = y[:, _OFF["cv_g"][0]:_OFF["cv_g"][1]]
    cu_ref[0] = (a * _sigmoid(g)).astype(BF16)
    plain("gt", gt_ref)


_INPROJ_WIDTHS = (512, 512, 512, 256, 256, 512, 512, 512, N_BRANCH * D_MODEL)


def _inproj(x, mods, mod_row, g_pre, w_in, layer, tables=None):
    b, n, d = x.shape
    tm = ATTN_BLOCK
    nb = n // tm
    rope = tables is not None
    token_major = lambda w: (pl.BlockSpec((1, tm, w), lambda bi, i: (bi, i, 0)), jax.ShapeDtypeStruct((b, n, w), BF16))
    outs = [(pl.BlockSpec((1, 512, tm), lambda bi, i: (bi, 0, i)), jax.ShapeDtypeStruct((b, 512, n), BF16)),
            token_major(512),
            (pl.BlockSpec((1, HEADS * V_ROWS, tm), lambda bi, i: (bi, 0, i)),
             jax.ShapeDtypeStruct((b, HEADS * V_ROWS, n), BF16))]
    outs += [token_major(w) for w in _INPROJ_WIDTHS[3:]]
    row = (lambda bi: bi) if mod_row is None else (lambda bi: mod_row)
    in_specs = [pl.BlockSpec((1, tm, d), lambda bi, i: (bi, i, 0)),
                pl.BlockSpec((1, N_ADA, d), lambda bi, i: (row(bi), 0, 0)),
                pl.BlockSpec((1, d), lambda bi, i: (0, 0)),
                _resident((None, d, D_IN), lambda bi, i: (layer, 0, 0))]
    args = [x, mods, g_pre, w_in]
    if rope:
        in_specs += [pl.BlockSpec((tm, 128), lambda bi, i: (i, 0))] * 2
        args += list(tables)
    return pl.pallas_call(
        functools.partial(_inproj_kernel, rope=rope),
        grid=(b, nb),
        in_specs=in_specs,
        out_specs=[o[0] for o in outs],
        out_shape=[o[1] for o in outs],
        name="inproj",
        compiler_params=_params(("parallel", "parallel")),
    )(*args)


def _rope_tables(n):
    t = jnp.arange(n, dtype=jnp.int32)
    row = (t // GRID_W).astype(F32)
    col = (t % GRID_W).astype(F32)
    axis_dim = ROPE_DIM // 2
    inv_freq = ROPE_BASE ** (-jnp.arange(0, axis_dim, 2, dtype=F32) / axis_dim)
    ang_r = row[:, None] * inv_freq
    ang_c = col[:, None] * inv_freq
    cos64 = jnp.concatenate([jnp.cos(ang_r), jnp.cos(ang_r), jnp.cos(ang_c), jnp.cos(ang_c)], axis=-1)
    sin64 = jnp.concatenate([-jnp.sin(ang_r), jnp.sin(ang_r), -jnp.sin(ang_c), jnp.sin(ang_c)], axis=-1)
    return jnp.tile(cos64, (1, 2)), jnp.tile(sin64, (1, 2))


def _attn_kernel(*refs, head, group, n_groups, q_tiles, lam_init):
    if n_groups:
        lam_ref, subln_ref, q_ref, kc_ref, vct_ref, kx_ref, vxt_ref, o_ref = refs[:8]
        scratch = refs[8:]
    else:
        lam_ref, subln_ref, q_ref, kc_ref, vct_ref, o_ref = refs[:6]
        scratch = refs[6:]
    s_refs = scratch[:ATTN_SLOTS]
    m_ref, mu_ref = scratch[ATTN_SLOTS:ATTN_SLOTS + 2]
    acc_refs = scratch[ATTN_SLOTS + 2:]
    tq = ATTN_BLOCK
    qmaps = []
    for t in range(q_tiles):
        qt = q_ref[0, :, tq * t:tq * (t + 1)]
        row = lax.broadcasted_iota(jnp.int32, qt.shape, 0)
        zero = jnp.zeros_like(qt)
        qmaps += [jnp.where(row < ROPE_DIM, qt, zero), jnp.where(row >= ROPE_DIM, qt, zero)]
    n_streams = len(qmaps)
    row0 = pl.multiple_of(jnp.minimum(pl.program_id(2), 0), ATTN_GROUP)

    def rows(size):
        return pl.ds(row0, size)

    n_units = n_groups + 1

    def latent_span(v):
        start = (v - 1) * group
        return pl.ds(start if isinstance(v, int) else pl.multiple_of(start, ATTN_BLOCK), group)

    def is_head(v):
        return isinstance(v, int) and v == 0

    def scores(v, slot):
        k, size = (kc_ref[0], head) if is_head(v) else (kx_ref[0, latent_span(v), :], group)
        for c in range(n_streams):
            s = jnp.dot(k, qmaps[c], preferred_element_type=F32)
            s_refs[slot][c, :size] = s
            mu_ref[slot, c] = jnp.max(s, axis=0, keepdims=True)

    def softmax_values(v, slot):
        first = is_head(v)
        vt, size = (vct_ref[0], head) if first else (vxt_ref[0, :, latent_span(v)], group)
        for c in range(n_streams):
            m_unit = mu_ref[slot, c]
            if first:
                m_new = m_unit
            else:
                m_old = m_ref[c]
                m_new = jnp.maximum(m_old, m_unit)
                alpha = jnp.exp2(m_old - m_new)
            m_ref[c] = m_new
            p = jnp.exp2(s_refs[slot][c, rows(size)] - m_new).astype(BF16)
            pv = jnp.dot(vt, p, preferred_element_type=F32)
            acc_refs[c][...] = pv if first else alpha * acc_refs[c][rows(V_ROWS)] + pv

    def step(v, cur):
        if not isinstance(v, int) or v + ATTN_LEAD < n_units:
            scores(v + ATTN_LEAD, (cur + ATTN_LEAD) % ATTN_SLOTS)
        softmax_values(v, cur)

    for v in range(min(ATTN_LEAD, n_units)):
        scores(v, v % ATTN_SLOTS)
    first_loop, end_loop = 1, n_units - ATTN_LEAD
    trips = max(end_loop - first_loop, 0) // ATTN_STEPS_PER_TRIP
    if trips == 1:
        trips = 0
    for v in range(min(first_loop, n_units)):
        step(v, v % ATTN_SLOTS)
    if trips:
        def body(i, carry):
            for t in range(ATTN_STEPS_PER_TRIP):
                step(first_loop + ATTN_STEPS_PER_TRIP * i + t, (first_loop + t) % ATTN_SLOTS)
            return carry

        lax.fori_loop(0, trips, body, 0)
    for v in range(first_loop + trips * ATTN_STEPS_PER_TRIP, n_units):
        step(v, v % ATTN_SLOTS)

    lv = lam_ref[...]
    lam = (jnp.exp(jnp.sum(lv[0:1] * lv[1:2], axis=-1, keepdims=True))
           - jnp.exp(jnp.sum(lv[2:3] * lv[3:4], axis=-1, keepdims=True)) + lam_init)
    hl = HEAD_LANES
    normed = [acc_ref[:hl] / acc_ref[hl:hl + 1] for acc_ref in acc_refs]
    for t in range(q_tiles):
        ot = normed[2 * t] - lam * normed[2 * t + 1]
        o_ref[0, tq * t:tq * (t + 1), :] = (_rms(ot.T, subln_ref[...]) * (1.0 - lam_init)).astype(BF16)


def _attention(q, kc, vct, kx, vxt, da_lambda, subln, lam_init):
    b, _, nq = q.shape
    hl = HEAD_LANES
    q_tiles = ATTN_Q_TILES if nq % (ATTN_Q_TILES * ATTN_BLOCK) == 0 else 1
    tq = q_tiles * ATTN_BLOCK
    n_streams = 2 * q_tiles
    head = kc.shape[1]
    rest = 0 if kx is None else kx.shape[1]
    group = ATTN_GROUP if rest % ATTN_GROUP == 0 else ATTN_BLOCK
    rows = max(group, head)
    keys_spec = lambda n_keys: [pl.BlockSpec((1, n_keys, hl), lambda bi, h, i: (bi, 0, h)),
                                pl.BlockSpec((1, V_ROWS, n_keys), lambda bi, h, i: (bi, h, 0))]
    in_specs = [pl.BlockSpec((4, ROPE_DIM), lambda bi, h, i: (0, 0)),
                pl.BlockSpec((1, hl), lambda bi, h, i: (0, 0)),
                pl.BlockSpec((1, hl, tq), lambda bi, h, i: (bi, h, i))] + keys_spec(head)
    args = [da_lambda, subln, q, kc, vct]
    if rest:
        in_specs += keys_spec(rest)
        args += [kx, vxt]
    return pl.pallas_call(
        functools.partial(_attn_kernel, head=head, group=group, n_groups=rest // group, q_tiles=q_tiles,
                          lam_init=lam_init),
        grid=(b, HEADS, nq // tq),
        in_specs=in_specs,
        out_specs=pl.BlockSpec((1, tq, hl), lambda bi, h, i: (bi, i, h)),
        out_shape=jax.ShapeDtypeStruct((b, nq, MIX_W), BF16),
        scratch_shapes=(
            [pltpu.VMEM((n_streams, rows, ATTN_BLOCK), F32)] * ATTN_SLOTS
            + [pltpu.VMEM((n_streams, 1, ATTN_BLOCK), F32)]
            + [pltpu.VMEM((ATTN_SLOTS, n_streams, 1, ATTN_BLOCK), F32)]
            + [pltpu.VMEM((V_ROWS, ATTN_BLOCK), F32)] * n_streams),
        name="attn",
        compiler_params=_params(("parallel", "parallel", "arbitrary")),
    )(*args)


def _ret_kernel(*refs, reverse, chunks):
    if reverse:
        (dec_ref, q_ref, k_ref, v_ref, s0_ref, of_ref, g_ref, gng_ref, gnb_ref,
         o_ref, sfin_ref, state_ref) = refs
    else:
        dec_ref, q_ref, k_ref, v_ref, s0_ref, o_ref, sfin_ref, state_ref = refs
    step = pl.program_id(0)
    batch = q_ref.shape[0]

    @pl.when(step == 0)
    def _():
        state_ref[...] = s0_ref[...]

    cch = RET_CHUNK
    ii = lax.broadcasted_iota(jnp.int32, (cch, cch), 0).astype(F32)
    ss = lax.broadcasted_iota(jnp.int32, (cch, cch), 1).astype(F32)
    pos = lax.broadcasted_iota(jnp.int32, (cch, 1), 0).astype(F32)
    lane = lax.broadcasted_iota(jnp.int32, (cch, 2 * ROPE_DIM), 1)
    d_row = 1 if reverse else 0
    for h in range(HEADS):
        x = dec_ref[d_row:d_row + 1, h:h + 1]
        lg = jnp.minimum(x, 0.0) - jnp.log(1.0 + jnp.exp(-jnp.abs(x)))
        if reverse:
            dist = ss - ii
            dmat = jnp.where(dist > 0, jnp.exp(lg * jnp.maximum(dist, 0.0)), 0.0)
            zeta = jnp.exp(lg * pos)
            xi = jnp.exp(lg * (cch - pos))
        else:
            dist = ii - ss
            dmat = jnp.where(dist >= 0, jnp.exp(lg * jnp.maximum(dist, 0.0)), 0.0)
            zeta = jnp.exp(lg * (cch - 1 - pos))
            xi = jnp.exp(lg * (pos + 1.0))
        chunk_decay = jnp.exp(lg * cch)
        pair = slice(128 * (h // 2), 128 * (h // 2 + 1))
        vcol = slice(128 * h, 128 * (h + 1))
        lo = ROPE_DIM * (h % 2)
        head_lanes = (lane >= lo) & (lane < lo + ROPE_DIM)
        order = range(chunks - 1, -1, -1) if reverse else range(chunks)
        for c in order:
            rows = slice(cch * c, cch * (c + 1))
            for bi in range(batch):
                qc = q_ref[bi, rows, pair]
                kc = k_ref[bi, rows, pair]
                kc = jnp.where(head_lanes, kc, jnp.zeros_like(kc))
                vc = v_ref[bi, rows, vcol]
                sc = lax.dot_general(qc, kc, (((1,), (1,)), ((), ())), preferred_element_type=F32) * dmat
                o = jnp.dot(sc.astype(BF16), vc, preferred_element_type=F32)
                state = state_ref[bi, h]
                o = o + xi * jnp.dot(qc, state.astype(BF16), preferred_element_type=F32)
                vz = (vc.astype(F32) * zeta).astype(BF16)
                kv = lax.dot_general(kc, vz, (((0,), (0,)), ((), ())), preferred_element_type=F32)
                state_ref[bi, h] = chunk_decay * state + kv
                if reverse:
                    t = o + of_ref[bi, rows, vcol]
                    mu = jnp.mean(t, axis=-1, keepdims=True)
                    tc = t - mu
                    var = jnp.mean(tc * tc, axis=-1, keepdims=True)
                    y = tc * lax.rsqrt(var + NORM_EPS) * gng_ref[:, vcol] + gnb_ref[:, vcol]
                    g = g_ref[bi, rows, vcol].astype(F32)
                    o_ref[bi, rows, vcol] = (y * (g * _sigmoid(g))).astype(BF16)
                else:
                    o_ref[bi, rows, vcol] = o

    @pl.when(step == pl.num_programs(0) - 1)
    def _():
        sfin_ref[...] = state_ref[...]


def _retention_pass(decay, q, k, v, s0, post=None):
    b, n, _ = q.shape
    t = next(c for c in (512, 256, RET_CHUNK) if n % c == 0)
    steps = n // t
    reverse = post is not None
    blk = (lambda i: (0, steps - 1 - i, 0)) if reverse else (lambda i: (0, i, 0))
    const2 = lambda i: (0, 0)
    state_spec = pl.BlockSpec((b, HEADS, 128, 128), lambda i: (0, 0, 0, 0))
    in_specs = [pl.BlockSpec((2, HEADS), const2),
                pl.BlockSpec((b, t, 256), blk), pl.BlockSpec((b, t, 256), blk), pl.BlockSpec((b, t, MIX_W), blk),
                state_spec]
    args = [decay, q, k, v, s0]
    if reverse:
        o_fwd, gate, gn_g, gn_b = post
        in_specs += [pl.BlockSpec((b, t, MIX_W), blk), pl.BlockSpec((b, t, MIX_W), blk),
                     pl.BlockSpec((1, MIX_W), const2), pl.BlockSpec((1, MIX_W), const2)]
        args += [o_fwd, gate, gn_g, gn_b]
    return pl.pallas_call(
        functools.partial(_ret_kernel, reverse=reverse, chunks=t // RET_CHUNK),
        grid=(steps,),
        in_specs=in_specs,
        out_specs=[pl.BlockSpec((b, t, MIX_W), blk), state_spec],
        out_shape=[jax.ShapeDtypeStruct((b, n, MIX_W), BF16 if reverse else F32),
                   jax.ShapeDtypeStruct((b, HEADS, 128, 128), F32)],
        scratch_shapes=[pltpu.VMEM((b, HEADS, 128, 128), F32)],
        name="retention_bwd" if reverse else "retention_fwd",
        compiler_params=_params(("arbitrary",)),
    )(*args)


def _conv_kernel(prev_ref, cur_ref, next_ref, dw_ref, dwb_ref, lng_ref, lnb_ref, o_ref, buf_ref, *, t):
    i = pl.program_id(1)
    halo = CONV_HALO
    first = i == 0
    last = i == pl.num_programs(1) - 1
    prev = prev_ref[0].astype(F32)
    nxt = next_ref[0].astype(F32)
    buf_ref[0, 0:halo, :] = jnp.where(first, jnp.zeros_like(prev), prev)
    buf_ref[0, halo:halo + t, :] = cur_ref[0].astype(F32)
    buf_ref[0, halo + t:2 * halo + t, :] = jnp.where(last, jnp.zeros_like(nxt), nxt)
    span = t + 2 * halo - F32_SUBLANES
    for r in range(1, F32_SUBLANES):
        buf_ref[r, 0:span, :] = buf_ref[0, r:r + span, :]
    rb = 32
    off = halo - CONV_WIDTH // 2
    for blk in range(t // rb):
        acc = jnp.broadcast_to(dwb_ref[...], (rb, CONV_CH))
        for w in range(CONV_WIDTH):
            r, base = (off + w) % F32_SUBLANES, rb * blk + (off + w) // F32_SUBLANES * F32_SUBLANES
            acc = acc + buf_ref[r, base:base + rb, :] * dw_ref[w:w + 1, :]
        mu = jnp.mean(acc, axis=-1, keepdims=True)
        ac = acc - mu
        var = jnp.mean(ac * ac, axis=-1, keepdims=True)
        y = ac * lax.rsqrt(var + NORM_EPS) * lng_ref[...] + lnb_ref[...]
        o_ref[0, rb * blk:rb * (blk + 1), :] = (y * _sigmoid(y)).astype(BF16)


def _conv(u, dw, dw_b, ln_g, ln_b):
    b, n, ch = u.shape
    t = min(256, n)
    hb = t // CONV_HALO
    nh = n // CONV_HALO
    const2 = lambda bi, i: (0, 0)
    return pl.pallas_call(
        functools.partial(_conv_kernel, t=t),
        grid=(b, n // t),
        in_specs=[pl.BlockSpec((1, CONV_HALO, ch), lambda bi, i: (bi, jnp.maximum(i * hb - 1, 0), 0)),
                  pl.BlockSpec((1, t, ch), lambda bi, i: (bi, i, 0)),
                  pl.BlockSpec((1, CONV_HALO, ch), lambda bi, i: (bi, jnp.minimum((i + 1) * hb, nh - 1), 0)),
                  pl.BlockSpec((CONV_WIDTH, ch), const2),
                  pl.BlockSpec((1, ch), const2), pl.BlockSpec((1, ch), const2), pl.BlockSpec((1, ch), const2)],
        out_specs=pl.BlockSpec((1, t, ch), lambda bi, i: (bi, i, 0)),
        out_shape=jax.ShapeDtypeStruct((b, n, ch), BF16),
        scratch_shapes=[pltpu.VMEM((F32_SUBLANES, t + 2 * CONV_HALO, ch), F32)],
        name="conv",
        compiler_params=_params(("parallel", "parallel")),
    )(u, u, u, dw, dw_b, ln_g, ln_b)


def _merge_kernel(x_ref, mod_ref, gpost_ref, oa_ref, yr_ref, yc_ref, gt_ref,
                  wa_ref, wr_ref, wc_ref, wo_ref, o_ref):
    d = D_MODEL
    gate = mod_ref[0, 5:6, :]
    tile = x_ref.shape[1]
    group = MERGE_ROWS if tile % MERGE_ROWS == 0 else tile
    for r0 in range(0, tile, group):
        rows = slice(r0, r0 + group)
        m = None
        for j, (y_ref, w_ref) in enumerate(((oa_ref, wa_ref), (yr_ref, wr_ref), (yc_ref, wc_ref))):
            y = jnp.dot(y_ref[0, rows], w_ref[...], preferred_element_type=F32)
            term = _sigmoid(gt_ref[0, rows, d * j:d * (j + 1)].astype(F32)) * y
            m = term if m is None else m + term
        y = jnp.dot(m.astype(BF16), wo_ref[...], preferred_element_type=F32)
        o_ref[0, rows] = x_ref[0, rows] + gate * _rms(y, gpost_ref[...])


def _merge(x, mods, mod_row, g_post, oa, yr, yc, gt, w_a, w_r, w_c, w_o, layer):
    b, n, d = x.shape
    tm = MERGE_TILE if n % MERGE_TILE == 0 else n
    row = (lambda bi: bi) if mod_row is None else (lambda bi: mod_row)
    tok = lambda w: pl.BlockSpec((1, tm, w), lambda bi, i: (bi, i, 0))
    wsp = lambda k: _resident((None, k, d), lambda bi, i: (layer, 0, 0))
    return pl.pallas_call(
        _merge_kernel,
        grid=(b, n // tm),
        in_specs=[tok(d), pl.BlockSpec((1, N_ADA, d), lambda bi, i: (row(bi), 0, 0)),
                  pl.BlockSpec((1, d), lambda bi, i: (0, 0)),
                  tok(MIX_W), tok(MIX_W), tok(MIX_W), tok(N_BRANCH * d),
                  wsp(MIX_W), wsp(MIX_W), wsp(MIX_W), wsp(d)],
        out_specs=tok(d),
        out_shape=jax.ShapeDtypeStruct(x.shape, F32),
        name="merge",
        compiler_params=_params(("parallel", "parallel")),
    )(x, mods, g_post, oa, yr, yc, gt, w_a, w_r, w_c, w_o)


def kernel(x, c, ctx, c_ctx, ada_w, ada_b, norm_pre, norm_post, ffn1_up, ffn1_down, ffn2_up, ffn2_down, w_in, da_lambda, da_subln, da_proj, ret_decay, ret_gn_g, ret_gn_b, ret_proj, conv_dw, conv_dw_b, conv_ln_g, conv_ln_b, conv_proj, w_out):
    depth = ada_w.shape[0]
    b, n, d = x.shape
    ctx_row = b
    cvec = jnp.zeros((8, d), F32).at[:b].set(c).at[ctx_row].set(c_ctx)
    mods = _ada(cvec, ada_w, ada_b).reshape(depth, 8, N_ADA, d)
    tables = _rope_tables(n)
    wb = lambda w: w.astype(BF16)
    ffn1_up, ffn1_down, ffn2_up, ffn2_down = wb(ffn1_up), wb(ffn1_down), wb(ffn2_up), wb(ffn2_down)
    w_in, da_proj, ret_proj, conv_proj, w_out = wb(w_in), wb(da_proj), wb(ret_proj), wb(conv_proj), wb(w_out)
    zero_state = jnp.zeros((b, HEADS, 128, 128), F32)

    for l in range(depth):
        lam_init = 0.8 - 0.6 * math.exp(-0.3 * l)
        mod_l = mods[l]
        row = lambda a, j: a[l, j][None, :]
        ctx = _ffn(ctx, mod_l, ctx_row, row(norm_pre, 0), row(norm_post, 0), ffn1_up, ffn1_down, l, 0)
        x = _ffn(x, mod_l, None, row(norm_pre, 0), row(norm_post, 0), ffn1_up, ffn1_down, l, 0)
        assert ctx.shape[1] == ATTN_BLOCK
        px = _inproj(x, mod_l, None, row(norm_pre, 1), w_in, l, tables=tables)
        pc = _inproj(ctx, mod_l, ctx_row, row(norm_pre, 1), w_in, l)
        subln = da_subln[l][None, :]
        dec = ret_decay[l]
        gn_g, gn_b = ret_gn_g[l][None, :], ret_gn_b[l][None, :]
        conv_args = (conv_dw[l], conv_dw_b[l][None, :], conv_ln_g[l][None, :], conv_ln_b[l][None, :])
        oa_c = _attention(pc[0], pc[1], pc[2], None, None, da_lambda[l], subln, lam_init)
        of_c, sf_c = _retention_pass(dec, pc[3], pc[4], pc[5], zero_state)
        yr_c, sb_c = _retention_pass(dec, pc[3], pc[4], pc[5], zero_state, post=(of_c, pc[6], gn_g, gn_b))
        yc_c = _conv(pc[7], *conv_args)
        oa_x = _attention(px[0], pc[1], pc[2], px[1], px[2], da_lambda[l], subln, lam_init)
        of_x, _ = _retention_pass(dec, px[3], px[4], px[5], sf_c)
        yr_x, _ = _retention_pass(dec, px[3], px[4], px[5], sb_c, post=(of_x, px[6], gn_g, gn_b))
        yc_x = _conv(px[7], *conv_args)
        outs = []
        for tok, mrow, oa, yr, yc, gt in ((ctx, ctx_row, oa_c, yr_c, yc_c, pc[8]), (x, None, oa_x, yr_x, yc_x, px[8])):
            tok = _merge(tok, mod_l, mrow, row(norm_post, 1), oa, yr, yc, gt, da_proj, ret_proj, conv_proj, w_out, l)
            tok = _ffn(tok, mod_l, mrow, row(norm_pre, 2), row(norm_post, 2), ffn2_up, ffn2_down, l, 6)
            outs.append(tok)
        ctx, x = outs
    return x
```
